```python
import math
import jax, jax.numpy as jnp
from jax import lax
import numpy as np

D_MODEL = 2048
BATCH = 2
SEQ = 4096
DEPTH = 4
DEC_BATCH = 8
DEC_SEQ = 4
PAST_LEN = 16384
PAGE_SIZE = 128

BRANCH_WIDTH = D_MODEL // 2
N_BRANCH = 3
MOBA_HEAD_DIM = 128
MOBA_HEADS = BRANCH_WIDTH // MOBA_HEAD_DIM
MOBA_WIDTH = MOBA_HEADS * MOBA_HEAD_DIM
MOBA_BLOCK = 256
MOBA_TOPK = 3
MOBA_Q_CHUNK = 64
HG_HEAD_DIM = 128
HG_HEADS = BRANCH_WIDTH // HG_HEAD_DIM
HG_WIDTH = HG_HEADS * HG_HEAD_DIM
HG_CHUNK = 64
MEM_TOKENS = 256
MEM_HEADS = 4
MEM_HEAD_DIM = BRANCH_WIDTH // MEM_HEADS
MEM_WIDTH = MEM_HEADS * MEM_HEAD_DIM
D_FF = 4 * D_MODEL
EPS = 1e-6
MASK_FILL = -1e30
IN_SPLIT = (MOBA_WIDTH, MOBA_WIDTH, MOBA_WIDTH, HG_WIDTH, HG_WIDTH, HG_WIDTH, HG_WIDTH, MEM_WIDTH, N_BRANCH * D_MODEL)
IN_WIDTH = sum(IN_SPLIT)

kernel_name = 'moba_hgrn2_memory_hybrid_step'


def rms_norm(x, g):
    xf = x.astype(jnp.float32)
    y = xf * lax.rsqrt(jnp.mean(xf * xf, axis=-1, keepdims=True) + EPS)
    return (y * g.astype(jnp.float32)).astype(x.dtype)


def heads(a, n):
    return a.reshape(a.shape[:-1] + (n, a.shape[-1] // n))


def layer_lower_bounds(lb_logits):
    p = jax.nn.softmax(lb_logits.astype(jnp.float32), axis=0)
    return jnp.cumsum(p, axis=0) - p[0]


def moba_attention(q, k_all, v_all, q_pos):
    B, T, H, dh = k_all.shape
    Lq = q.shape[1]
    f32 = jnp.float32
    n_blk = -(-T // MOBA_BLOCK)
    pad = n_blk * MOBA_BLOCK - T

    def blocks(a):
        a = jnp.pad(a, ((0, 0), (0, pad), (0, 0), (0, 0)))
        return a.reshape(B, n_blk, MOBA_BLOCK, H, dh).transpose(0, 3, 1, 2, 4)

    kb, vb = blocks(k_all), blocks(v_all)
    k_mean = jnp.mean(kb.astype(f32), axis=3)
    k_sel = min(MOBA_TOPK, n_blk - 1)
    qc = math.gcd(Lq, MOBA_Q_CHUNK)
    nq = Lq // qc
    q_chunks = q.reshape(B, nq, qc, H, dh).transpose(1, 0, 3, 2, 4)
    pos_chunks = q_pos.reshape(nq, qc)
    b_ix = jnp.arange(B)[:, None, None, None]
    h_ix = jnp.arange(H)[None, :, None, None]
    blk_off = jnp.arange(MOBA_BLOCK)
    scale = dh ** -0.5

    def attend_chunk(args):
        qch, pos = args
        qf = qch.astype(f32)
        own = pos // MOBA_BLOCK
        own_b = jnp.broadcast_to(own[None, None, :, None], (B, H, qc, 1))
        if k_sel > 0:
            score = jnp.einsum('bhqd,bhnd->bhqn', qf, k_mean)
            past = jnp.arange(n_blk)[None, :] < own[:, None]
            score = jnp.where(past, score, MASK_FILL)
            _, top = lax.top_k(score, k_sel)
            idx = jnp.concatenate([top, own_b], axis=-1)
            slot_ok = jnp.concatenate([top < own[None, None, :, None], jnp.ones_like(own_b, dtype=bool)], axis=-1)
        else:
            idx = own_b
            slot_ok = jnp.ones_like(own_b, dtype=bool)
        J = idx.shape[-1]
        ksel = kb[b_ix, h_ix, idx].astype(f32)
        vsel = vb[b_ix, h_ix, idx].astype(f32)
        logits = jnp.einsum('bhqd,bhqjnd->bhqjn', qf, ksel) * scale
        key_pos = idx[..., None] * MOBA_BLOCK + blk_off
        ok = slot_ok[..., None] & (key_pos <= pos[None, None, :, None, None])
        logits = jnp.where(ok, logits, MASK_FILL).reshape(B, H, qc, J * MOBA_BLOCK)
        p = jax.nn.softmax(logits, axis=-1)
        out = jnp.einsum('bhqm,bhqmd->bqhd', p, vsel.reshape(B, H, qc, J * MOBA_BLOCK, dh))
        return out.astype(q.dtype)

    out = lax.map(attend_chunk, (q_chunks, pos_chunks))
    return out.transpose(1, 0, 2, 3, 4).reshape(B, Lq, H, dh)


def hgrn2_recurrence(q, k, log_f, v, s0):
    B, L, H, dk = q.shape
    dv = v.shape[-1]
    C = math.gcd(L, HG_CHUNK)
    n = L // C

    def chunks(a):
        return a.reshape(B, n, C, H, a.shape[-1]).transpose(1, 0, 3, 2, 4)

    causal = jnp.tril(jnp.ones((C, C), dtype=bool))[:, :, None]

    def step(S, inp):
        qc, kc, gc, vc = inp
        b = jnp.cumsum(gc, axis=2)
        b_end = b[:, :, -1:, :]
        diff = b[:, :, :, None, :] - b[:, :, None, :, :]
        decay = jnp.where(causal, jnp.exp(jnp.where(causal, diff, 0.0)), 0.0)
        attn = jnp.einsum('bhtd,bhsd,bhtsd->bhts', qc, kc, decay)
        o = jnp.einsum('bhts,bhsv->bhtv', attn, vc) + jnp.einsum('bhtd,bhdv->bhtv', qc * jnp.exp(b), S)
        S = jnp.exp(b_end[:, :, 0, :, None]) * S + jnp.einsum('bhsd,bhsv->bhdv', kc * jnp.exp(b_end - b), vc)
        return S, o

    S, o = lax.scan(step, s0, (chunks(q), chunks(k), chunks(log_f), chunks(v)))
    return o.transpose(1, 0, 3, 2, 4).reshape(B, L, H, dv), S


def memory_kv(mem, ln_g, w_kv, kn_g):
    m = rms_norm(mem, ln_g)
    k, v = jnp.split(m @ w_kv, 2, axis=-1)
    return rms_norm(heads(k, MEM_HEADS), kn_g), heads(v, MEM_HEADS)


def memory_attention(q, mem_k, mem_v):
    f32 = jnp.float32
    s = jnp.einsum('blhd,bmhd->bhlm', q.astype(f32), mem_k.astype(f32)) * (MEM_HEAD_DIM ** -0.5)
    p = jax.nn.softmax(s, axis=-1)
    return jnp.einsum('bhlm,bmhd->blhd', p, mem_v.astype(f32)).astype(q.dtype)


def trunk_layer(x, pos, past_k, past_v, mem_k, mem_v, s0, lb, ln_mix, w_in, qn_a, kn_a, hg_norm, qn_m,
                w_branch, w_out, ln_mlp, w_up, w_down):
    B, L, _ = x.shape
    f32 = jnp.float32
    h = rms_norm(x, ln_mix)
    offs = np.cumsum(IN_SPLIT)[:-1].tolist()
    qa, ka, va, qh, fh, ih, gh, qm, gate = jnp.split(h @ w_in, offs, axis=-1)
    qa = rms_norm(heads(qa, MOBA_HEADS), qn_a)
    ka = rms_norm(heads(ka, MOBA_HEADS), kn_a)
    va = heads(va, MOBA_HEADS)
    if past_k is None:
        k_all, v_all = ka, va
    else:
        k_all = jnp.concatenate([past_k.astype(ka.dtype), ka], axis=1)
        v_all = jnp.concatenate([past_v.astype(va.dtype), va], axis=1)
    o_a = moba_attention(qa, k_all, v_all, pos)
    lbh = heads(lb, HG_HEADS)
    f_gate = lbh + (1.0 - lbh) * jax.nn.sigmoid(heads(fh, HG_HEADS).astype(f32))
    log_f = jnp.log(f_gate)
    k_h = 1.0 - f_gate
    q_h = jax.nn.silu(heads(qh, HG_HEADS).astype(f32))
    o_h, s_new = hgrn2_recurrence(q_h, k_h, log_f, heads(ih, HG_HEADS).astype(f32), s0.astype(f32))
    o_h = (rms_norm(o_h, hg_norm) * jax.nn.sigmoid(heads(gh, HG_HEADS).astype(f32))).astype(x.dtype)
    o_m = memory_attention(rms_norm(heads(qm, MEM_HEADS), qn_m), mem_k, mem_v)
    br = jnp.stack([o_a.reshape(B, L, BRANCH_WIDTH), o_h.reshape(B, L, BRANCH_WIDTH),
                    o_m.reshape(B, L, BRANCH_WIDTH)], axis=2)
    up = jnp.einsum('blnc,ncd->blnd', br, w_branch)
    g = jax.nn.sigmoid(gate.reshape(B, L, N_BRANCH, D_MODEL))
    x = x + jnp.sum(g * up, axis=2) @ w_out
    h2 = rms_norm(x, ln_mlp)
    x = x + jnp.square(jax.nn.relu(h2 @ w_up)) @ w_down
    return x, ka, va, s_new.astype(s0.dtype)


def setup_inputs(seed: int = 0) -> dict:
    key = jax.random.key(seed)
    k = jax.random.split(key, 24)
    f32 = jnp.float32
    n_pages = PAST_LEN // PAGE_SIZE
    n_used = DEC_BATCH * n_pages
    n_pool = n_used + max(1, n_used // 4)
    page_table = jax.random.permutation(k[0], n_pool)[:n_used].reshape(DEC_BATCH, n_pages).astype(jnp.int32)

    def nrm(i, shape, scale=1.0):
        return jax.random.normal(k[i], shape, f32) * scale

    def gain(i, shape):
        return 1.0 + 0.02 * jax.random.normal(k[i], shape, f32)

    kv_shape = (DEPTH, n_pool, PAGE_SIZE, MOBA_HEADS, MOBA_HEAD_DIM)
    mem_shape = (DEPTH, DEC_BATCH, MEM_TOKENS, MEM_HEADS, MEM_HEAD_DIM)
    return {
        'x_prompt': nrm(1, (BATCH, SEQ, D_MODEL)),
        'x_sample': nrm(2, (DEC_BATCH, DEC_SEQ, D_MODEL)),
        'mem_prompt': nrm(3, (BATCH, MEM_TOKENS, D_MODEL)),
        'cache_moba_k': nrm(4, kv_shape),
        'cache_moba_v': nrm(5, kv_shape),
        'cache_mem_k': nrm(6, mem_shape),
        'cache_mem_v': nrm(7, mem_shape),
        'state_hgrn': nrm(8, (DEPTH, DEC_BATCH, HG_HEADS, HG_HEAD_DIM, HG_HEAD_DIM), 0.5),
        'page_table': page_table,
        'ln_mix': gain(9, (DEPTH, D_MODEL)),
        'w_in': nrm(10, (DEPTH, D_MODEL, IN_WIDTH), D_MODEL ** -0.5),
        'q_norm_moba': gain(11, (DEPTH, MOBA_HEAD_DIM)),
        'k_norm_moba': gain(12, (DEPTH, MOBA_HEAD_DIM)),
        'lb_logits': nrm(13, (DEPTH, HG_WIDTH), 0.5),
        'hg_out_norm': gain(14, (DEPTH, HG_HEAD_DIM)),
        'q_norm_mem': gain(15, (DEPTH, MEM_HEAD_DIM)),
        'ln_mem': gain(16, (DEPTH, D_MODEL)),
        'w_mem_kv': nrm(17, (DEPTH, D_MODEL, 2 * MEM_WIDTH), D_MODEL ** -0.5),
        'k_norm_mem': gain(18, (DEPTH, MEM_HEAD_DIM)),
        'w_branch': nrm(19, (DEPTH, N_BRANCH, BRANCH_WIDTH, D_MODEL), BRANCH_WIDTH ** -0.5),
        'w_out': nrm(20, (DEPTH, D_MODEL, D_MODEL), D_MODEL ** -0.5),
        'ln_mlp': gain(21, (DEPTH, D_MODEL)),
        'w_up': nrm(22, (DEPTH, D_MODEL, D_FF), D_MODEL ** -0.5),
        'w_down': nrm(23, (DEPTH, D_FF, D_MODEL), D_FF ** -0.5),
    }


def reference(x_prompt, x_sample, mem_prompt, cache_moba_k, cache_moba_v, cache_mem_k, cache_mem_v, state_hgrn,
              page_table, ln_mix, w_in, q_norm_moba, k_norm_moba, lb_logits, hg_out_norm, q_norm_mem, ln_mem,
              w_mem_kv, k_norm_mem, w_branch, w_out, ln_mlp, w_up, w_down):
    lower = layer_lower_bounds(lb_logits)
    n_dec, n_new = x_sample.shape[0], x_sample.shape[1]
    past_len = page_table.shape[1] * PAGE_SIZE
    pos_p = jnp.arange(x_prompt.shape[1], dtype=jnp.int32)
    pos_s = past_len + jnp.arange(n_new, dtype=jnp.int32)
    s0_p = jnp.zeros((x_prompt.shape[0], HG_HEADS, HG_HEAD_DIM, HG_HEAD_DIM), state_hgrn.dtype)
    xp, xs = x_prompt, x_sample
    kp_l, vp_l, mkp_l, mvp_l, sp_l, ks_l, vs_l, ss_l = [], [], [], [], [], [], [], []
    for l in range(DEPTH):
        lw = (ln_mix[l], w_in[l], q_norm_moba[l], k_norm_moba[l], hg_out_norm[l], q_norm_mem[l],
              w_branch[l], w_out[l], ln_mlp[l], w_up[l], w_down[l])
        mk, mv = memory_kv(mem_prompt, ln_mem[l], w_mem_kv[l], k_norm_mem[l])
        xp, kp, vp, sp = trunk_layer(xp, pos_p, None, None, mk, mv, s0_p, lower[l], *lw)
        past_k = cache_moba_k[l][page_table].reshape(n_dec, past_len, MOBA_HEADS, MOBA_HEAD_DIM)
        past_v = cache_moba_v[l][page_table].reshape(n_dec, past_len, MOBA_HEADS, MOBA_HEAD_DIM)
        xs, ks, vs, ss = trunk_layer(xs, pos_s, past_k, past_v, cache_mem_k[l], cache_mem_v[l], state_hgrn[l],
                                     lower[l], *lw)
        kp_l.append(kp); vp_l.append(vp); mkp_l.append(mk); mvp_l.append(mv); sp_l.append(sp)
        ks_l.append(ks); vs_l.append(vs); ss_l.append(ss)
    return (xp, xs, jnp.stack(kp_l), jnp.stack(vp_l), jnp.stack(mkp_l), jnp.stack(mvp_l), jnp.stack(sp_l),
            jnp.stack(ks_l), jnp.stack(vs_l), jnp.stack(ss_l))
```

```python
import functools

import jax
import jax.numpy as jnp
from jax import lax
from jax.experimental import pallas as pl
from jax.experimental.pallas import tpu as pltpu

F32 = jnp.float32
BF16 = jnp.bfloat16

EPS = 1e-6
MASK_FILL = -1e30
MOBA_BLOCK = 256
MOBA_TOPK = 3
HEAD_DIM = 128
MEM_HEADS = 4
HG_CHUNK = 64
HG_SUB = 16
HG_STEP = 256
PAGES_PER_STEP = 8
VMEM_LIMIT_BYTES = 56 * 1024 * 1024

_NT = (((1,), (1,)), ((), ()))
_TN = (((0,), (0,)), ((), ()))


def _params(*sem):
    return pltpu.CompilerParams(dimension_semantics=sem, vmem_limit_bytes=VMEM_LIMIT_BYTES)


def _sigmoid(x):
    return 1.0 / (1.0 + jnp.exp(-x))


def _lower_bounds_kernel(lg_ref, out_ref):
    x = lg_ref[...]
    e = jnp.exp(x - jnp.max(x, axis=0, keepdims=True))
    p = e / jnp.sum(e, axis=0, keepdims=True)
    c = p[0:1]
    rows = [c - p[0:1]]
    for l in range(1, x.shape[0]):
        c = c + p[l:l + 1]
        rows.append(c - p[0:1])
    out_ref[...] = jnp.concatenate(rows, axis=0)


def _lower_bounds(lb_logits):
    return pl.pallas_call(
        _lower_bounds_kernel,
        out_shape=jax.ShapeDtypeStruct(lb_logits.shape, F32),
        name="lower_bounds",
    )(lb_logits)


def _rmsnorm_kernel(x_ref, g_ref, o_ref):
    x = x_ref[...]
    y = x * lax.rsqrt(jnp.mean(x * x, axis=-1, keepdims=True) + EPS)
    o_ref[...] = (y * g_ref[...]).astype(o_ref.dtype)


def _rmsnorm_bf16(x, g, tm):
    n, d = x.shape
    return pl.pallas_call(
        _rmsnorm_kernel,
        grid=(n // tm,),
        in_specs=[pl.BlockSpec((tm, d), lambda i: (i, 0)), pl.BlockSpec((1, d), lambda i: (0, 0))],
        out_specs=pl.BlockSpec((tm, d), lambda i: (i, 0)),
        out_shape=jax.ShapeDtypeStruct((n, d), BF16),
        compiler_params=_params("parallel"),
        name="rmsnorm",
    )(x, g.reshape(1, d))


def _proj_kernel(h_ref, w_ref, g_ref, o_ref, *, n_norm_tiles, group):
    acc = jnp.dot(h_ref[...], w_ref[...], preferred_element_type=F32)
    j = pl.program_id(1)

    @pl.when(j < n_norm_tiles)
    def _():
        for c in range(acc.shape[1] // group):
            a = acc[:, c * group:(c + 1) * group]
            y = a * lax.rsqrt(jnp.mean(a * a, axis=-1, keepdims=True) + EPS)
            o_ref[:, c * group:(c + 1) * group] = y * g_ref[:, c * group:(c + 1) * group]

    @pl.when(j >= n_norm_tiles)
    def _():
        o_ref[...] = acc


def _proj(h, w_all, layer, gains, group, tm, tn):
    n, k = h.shape
    m = w_all.shape[2]
    n_norm_tiles = gains.shape[1] // tn
    kern = functools.partial(_proj_kernel, n_norm_tiles=n_norm_tiles, group=group)
    return pl.pallas_call(
        kern,
        grid=(n // tm, m // tn),
        in_specs=[
            pl.BlockSpec((tm, k), lambda i, j: (i, 0)),
            pl.BlockSpec((None, k, tn), lambda i, j: (layer, 0, j)),
            pl.BlockSpec((1, tn), lambda i, j: (0, jnp.minimum(j, n_norm_tiles - 1))),
        ],
        out_specs=pl.BlockSpec((tm, tn), lambda i, j: (i, j)),
        out_shape=jax.ShapeDtypeStruct((n, m), F32),
        compiler_params=_params("parallel", "arbitrary"),
        name="proj",
    )(h, w_all, gains)


def _moba_prompt_kernel(q_ref, k_ref, v_ref, o_ref, kbf, vt, kmean, selrow, *, nblk, k_sel, scale):
    i = pl.program_id(2)
    blk = MOBA_BLOCK

    @pl.when(i == 0)
    def _():
        for j in range(nblk):
            kj = k_ref[j * blk:(j + 1) * blk, :]
            kbf[j * blk:(j + 1) * blk, :] = kj.astype(BF16)
            kmean[j:j + 1, :] = jnp.mean(kj, axis=0, keepdims=True)
            vt[j] = v_ref[j * blk:(j + 1) * blk, :].T.astype(BF16)

    q = q_ref[...]
    qb = q.astype(BF16)

    sc = lax.dot_general(kmean[...], q, _NT, precision=lax.Precision.HIGHEST,
                         preferred_element_type=F32)
    jj = lax.broadcasted_iota(jnp.int32, sc.shape, 0)
    past = jj < i
    sc = jnp.where(past, sc, MASK_FILL)
    rank = jnp.zeros(sc.shape, jnp.int32)
    for j2 in range(nblk):
        sj = sc[j2:j2 + 1, :]
        beats = (sj > sc) | ((sj == sc) & (j2 < jj))
        rank = rank + beats.astype(jnp.int32)
    selrow[...] = jnp.where(past & (rank < k_sel), 1.0, 0.0)

    def logits_t(j):
        kj = kbf[pl.ds(pl.multiple_of(j * blk, blk), blk), :]
        return lax.dot_general(kj, qb, _NT, preferred_element_type=F32) * scale

    lt = logits_t(i)
    kpos = lax.broadcasted_iota(jnp.int32, lt.shape, 0)
    qpos = lax.broadcasted_iota(jnp.int32, lt.shape, 1)
    lt = jnp.where(kpos <= qpos, lt, MASK_FILL)
    m0 = jnp.max(lt, axis=0, keepdims=True)
    p = jnp.exp(lt - m0)
    l0 = jnp.sum(p, axis=0, keepdims=True)
    acc0 = jnp.dot(vt[i], p.astype(BF16), preferred_element_type=F32)

    def body(j, carry):
        m, l, acc = carry
        lt = logits_t(j)
        lt = jnp.where(jnp.broadcast_to(selrow[pl.ds(j, 1), :], lt.shape) > 0.5, lt, MASK_FILL)
        m_new = jnp.maximum(m, jnp.max(lt, axis=0, keepdims=True))
        alpha = jnp.exp(m - m_new)
        p = jnp.exp(lt - m_new)
        l = alpha * l + jnp.sum(p, axis=0, keepdims=True)
        acc = alpha * acc + jnp.dot(vt[j], p.astype(BF16), preferred_element_type=F32)
        return m_new, l, acc

    _, l, acc = lax.fori_loop(0, i, body, (m0, l0, acc0))
    o_ref[...] = (acc / l).T.astype(o_ref.dtype)


def _moba_prompt(p3, n_heads):
    b, l, _ = p3.shape
    assert l % MOBA_BLOCK == 0
    nblk = l // MOBA_BLOCK
    dh = HEAD_DIM
    kern = functools.partial(_moba_prompt_kernel, nblk=nblk, k_sel=min(MOBA_TOPK, nblk - 1), scale=dh ** -0.5)
    return pl.pallas_call(
        kern,
        grid=(b, n_heads, nblk),
        in_specs=[
            pl.BlockSpec((None, MOBA_BLOCK, dh), lambda bb, h, i: (bb, i, h)),
            pl.BlockSpec((None, l, dh), lambda bb, h, i: (bb, 0, n_heads + h)),
            pl.BlockSpec((None, l, dh), lambda bb, h, i: (bb, 0, 2 * n_heads + h)),
        ],
        out_specs=pl.BlockSpec((None, MOBA_BLOCK, dh), lambda bb, h, i: (bb, i, h)),
        out_shape=jax.ShapeDtypeStruct((b, l, n_heads * dh), BF16),
        scratch_shapes=[
            pltpu.VMEM((l, dh), BF16),
            pltpu.VMEM((nblk, dh, MOBA_BLOCK), BF16),
            pltpu.VMEM((nblk, dh), F32),
            pltpu.VMEM((nblk, MOBA_BLOCK), F32),
        ],
        compiler_params=_params("parallel", "parallel", "arbitrary"),
        name="moba_prompt",
    )(p3, p3, p3)


def _moba_sample_kernel(pt_ref, q_ref, kn_ref, vn_ref, *rest, n_steps, n_heads, n_new, page, k_sel, scale):
    g = PAGES_PER_STEP
    k_refs = rest[:g]
    v_refs = rest[g:2 * g]
    o_ref = rest[2 * g]
    qblk, s_all, ksum, pbuf, newpad, acc, lsum = rest[2 * g + 1:]
    ph = pl.program_id(1)
    st = pl.program_id(2)
    rows = n_new * n_heads
    width = n_heads * HEAD_DIM
    ppb = MOBA_BLOCK // page
    n_pages = n_steps * g
    n_past = n_pages // ppb

    def head_mask(shape):
        r = lax.broadcasted_iota(jnp.int32, shape, 0)
        c = lax.broadcasted_iota(jnp.int32, shape, 1)
        return (r % n_heads) == (c // HEAD_DIM)

    @pl.when((ph == 0) & (st == 0))
    def _():
        q = q_ref[...]
        parts = [jnp.broadcast_to(q[t:t + 1, :], (n_heads, width)) for t in range(n_new)]
        qb = jnp.concatenate(parts, axis=0)
        qblk[...] = jnp.where(head_mask(qb.shape), qb, 0.0)

    @pl.when(ph == 0)
    def _():
        qb16 = qblk[...].astype(BF16)
        for u in range(g):
            kp = k_refs[u][...]
            pidx = st * g + u
            s_all[pidx] = lax.dot_general(qb16, kp.astype(BF16), _NT, preferred_element_type=F32) * scale
            srow = jnp.sum(kp, axis=0, keepdims=True)
            bidx = st * (g // ppb) + u // ppb
            if u % ppb == 0:
                ksum[pl.ds(bidx, 1), :] = srow
            else:
                ksum[pl.ds(bidx, 1), :] = ksum[pl.ds(bidx, 1), :] + srow

    @pl.when((ph == 1) & (st == 0))
    def _():
        kmean = ksum[...] * (1.0 / MOBA_BLOCK)
        sc = lax.dot_general(qblk[...], kmean, _NT, precision=lax.Precision.HIGHEST,
                             preferred_element_type=F32)
        jj = lax.broadcasted_iota(jnp.int32, sc.shape, 1)
        rank = jnp.zeros(sc.shape, jnp.int32)
        for j2 in range(n_past):
            sj = sc[:, j2:j2 + 1]
            beats = (sj > sc) | ((sj == sc) & (j2 < jj))
            rank = rank + beats.astype(jnp.int32)
        sel = jnp.where(rank < k_sel, 1.0, 0.0)

        def selected(pidx):
            col = sel[:, pidx // ppb:pidx // ppb + 1]
            return jnp.where(jnp.broadcast_to(col, (rows, page)) > 0.5, s_all[pidx], MASK_FILL)

        newpad[...] = jnp.zeros(newpad.shape, F32)
        newpad[0:n_new, :] = kn_ref[...]
        s_new = lax.dot_general(qblk[...].astype(BF16), newpad[...].astype(BF16), _NT,
                                preferred_element_type=F32) * scale
        r = lax.broadcasted_iota(jnp.int32, s_new.shape, 0)
        c = lax.broadcasted_iota(jnp.int32, s_new.shape, 1)
        s_new = jnp.where(c <= r // n_heads, s_new, MASK_FILL)

        mx = s_new
        for pidx in range(n_pages):
            mx = jnp.maximum(mx, selected(pidx))
        m = jnp.max(mx, axis=-1, keepdims=True)
        p_new = jnp.exp(s_new - m)
        ls = p_new
        for pidx in range(n_pages):
            pp = jnp.exp(selected(pidx) - m)
            ls = ls + pp
            pbuf[pidx] = pp.astype(BF16)
        lsum[...] = jnp.sum(ls, axis=-1, keepdims=True)
        newpad[...] = jnp.zeros(newpad.shape, F32)
        newpad[0:n_new, :] = vn_ref[...]
        acc[...] = jnp.dot(p_new.astype(BF16), newpad[...].astype(BF16), preferred_element_type=F32)

    @pl.when(ph == 1)
    def _():
        a = acc[...]
        for u in range(g):
            a = a + jnp.dot(pbuf[st * g + u], v_refs[u][...].astype(BF16), preferred_element_type=F32)
        acc[...] = a

    @pl.when((ph == 1) & (st == n_steps - 1))
    def _():
        on = acc[...] / lsum[...]
        on = jnp.where(head_mask(on.shape), on, 0.0)
        for t in range(n_new):
            o_ref[t:t + 1, :] = jnp.sum(on[t * n_heads:(t + 1) * n_heads, :], axis=0, keepdims=True)


def _moba_sample(p3, cache_k, cache_v, layer, page_table, n_heads):
    b, n_new, _ = p3.shape
    page = cache_k.shape[2]
    width = n_heads * HEAD_DIM
    n_pages = page_table.shape[1]
    g = PAGES_PER_STEP
    ppb = MOBA_BLOCK // page
    assert MOBA_BLOCK % page == 0 and g % ppb == 0 and n_pages % g == 0 and n_new <= page
    n_steps = n_pages // g
    n_past = n_pages // ppb
    rows = n_new * n_heads
    kern = functools.partial(_moba_sample_kernel, n_steps=n_steps, n_heads=n_heads, n_new=n_new, page=page,
                             k_sel=min(MOBA_TOPK, n_past), scale=HEAD_DIM ** -0.5)

    def k_map(u):
        return lambda bb, ph, st, pt: (layer, pt[bb, jnp.where(ph == 0, st, n_steps - 1) * g + u], 0, 0)

    def v_map(u):
        return lambda bb, ph, st, pt: (layer, pt[bb, jnp.where(ph == 0, 0, st) * g + u], 0, 0)

    new_spec = lambda col: pl.BlockSpec((None, n_new, width), lambda bb, ph, st, pt: (bb, 0, col))
    page_spec = lambda imap: pl.BlockSpec((None, None, page, width), imap)
    grid_spec = pltpu.PrefetchScalarGridSpec(
        num_scalar_prefetch=1,
        grid=(b, 2, n_steps),
        in_specs=[new_spec(0), new_spec(1), new_spec(2)]
                 + [page_spec(k_map(u)) for u in range(g)] + [page_spec(v_map(u)) for u in range(g)],
        out_specs=pl.BlockSpec((None, n_new, width), lambda bb, ph, st, pt: (bb, 0, 0)),
        scratch_shapes=[
            pltpu.VMEM((rows, width), F32),
            pltpu.VMEM((n_pages, rows, page), F32),
            pltpu.VMEM((n_past, width), F32),
            pltpu.VMEM((n_pages, rows, page), BF16),
            pltpu.VMEM((page, width), F32),
            pltpu.VMEM((rows, width), F32),
            pltpu.VMEM((rows, 1), F32),
        ],
    )
    return pl.pallas_call(
        kern,
        grid_spec=grid_spec,
        out_shape=jax.ShapeDtypeStruct((b, n_new, width), F32),
        compiler_params=_params("parallel", "arbitrary", "arbitrary"),
        name="moba_sample",
    )(page_table, p3, p3, p3, *([cache_k] * g), *([cache_v] * g))


def _hgrn_chunk(xq, xf, xi, lb, st_t, tril):
    c, sub = HG_CHUNK, HG_SUB
    f = lb + (1.0 - lb) * _sigmoid(xf)
    gl = jnp.log(f)
    k = 1.0 - f
    q = xq * _sigmoid(xq)
    v = xi
    b = jnp.dot(tril, gl, precision=lax.Precision.HIGHEST, preferred_element_type=F32)
    b_end = b[c - 1:c, :]
    vb = v.astype(BF16)

    o_inter = lax.dot_general((q * jnp.exp(b)).astype(BF16), st_t.astype(BF16), _NT, preferred_element_type=F32)
    kd = (k * jnp.exp(b_end - b)).astype(BF16)
    st_new = st_t * jnp.exp(b_end) + lax.dot_general(vb, kd, _TN, preferred_element_type=F32)

    tio = lax.broadcasted_iota(jnp.int32, (sub, 1), 0)
    outs = []
    for blk in range(c // sub):
        r0 = blk * sub
        qi, ki, bi, vi = q[r0:r0 + sub], k[r0:r0 + sub], b[r0:r0 + sub], v[r0:r0 + sub]
        oi = o_inter[r0:r0 + sub]
        if blk > 0:
            bref = b[r0 - 1:r0, :]
            qs = (qi * jnp.exp(bi - bref)).astype(BF16)
            ks = (k[:r0] * jnp.exp(bref - b[:r0])).astype(BF16)
            a = lax.dot_general(qs, ks, _NT, preferred_element_type=F32)
            oi = oi + jnp.dot(a.astype(BF16), vb[:r0], preferred_element_type=F32)
        for s in range(sub):
            w = qi * ki[s:s + 1] * jnp.exp(jnp.minimum(bi - bi[s:s + 1], 0.0))
            a = jnp.sum(w, axis=-1, keepdims=True)
            oi = oi + jnp.where(tio >= s, a, 0.0) * vi[s:s + 1]
        outs.append(oi)
    return jnp.concatenate(outs, axis=0), st_new


def _hgrn_prompt_kernel(xq_ref, xf_ref, xi_ref, xg_ref, lb_ref, gn_ref, s0_ref, o_ref, s_ref, st_scr):
    t = pl.program_id(2)
    c = HG_CHUNK

    @pl.when(t == 0)
    def _():
        st_scr[...] = s0_ref[...].T

    r = lax.broadcasted_iota(jnp.int32, (c, c), 0)
    cc = lax.broadcasted_iota(jnp.int32, (c, c), 1)
    tril = jnp.where(cc <= r, 1.0, 0.0).astype(F32)
    lb = lb_ref[...]
    gn = gn_ref[...]

    def body(ci, st_t):
        rows = pl.ds(pl.multiple_of(ci * c, c), c)
        o, st_t = _hgrn_chunk(xq_ref[rows, :], xf_ref[rows, :], xi_ref[rows, :], lb, st_t, tril)
        y = o * lax.rsqrt(jnp.mean(o * o, axis=-1, keepdims=True) + EPS) * gn
        o_ref[rows, :] = (y * _sigmoid(xg_ref[rows, :])).astype(o_ref.dtype)
        return st_t

    st_t = lax.fori_loop(0, xq_ref.shape[0] // c, body, st_scr[...])
    st_scr[...] = st_t

    @pl.when(t == pl.num_programs(2) - 1)
    def _():
        s_ref[...] = st_t.T


def _hgrn_prompt(p3, lb, gn, s0, n_heads, col0):
    b, l, _ = p3.shape
    dh = HEAD_DIM
    step = min(HG_STEP, l)
    assert l % step == 0 and step % HG_CHUNK == 0
    c0 = col0 // dh
    spec = lambda off: pl.BlockSpec((None, step, dh), lambda bb, h, t: (bb, t, c0 + off * n_heads + h))
    vec = pl.BlockSpec((1, dh), lambda bb, h, t: (0, h))
    state = pl.BlockSpec((None, None, dh, dh), lambda bb, h, t: (bb, h, 0, 0))
    return pl.pallas_call(
        _hgrn_prompt_kernel,
        grid=(b, n_heads, l // step),
        in_specs=[spec(0), spec(1), spec(2), spec(3), vec, pl.BlockSpec((1, dh), lambda bb, h, t: (0, 0)), state],
        out_specs=[pl.BlockSpec((None, step, dh), lambda bb, h, t: (bb, t, h)), state],
        out_shape=[jax.ShapeDtypeStruct((b, l, n_heads * dh), BF16), jax.ShapeDtypeStruct(s0.shape, F32)],
        scratch_shapes=[pltpu.VMEM((dh, dh), F32)],
        compiler_params=_params("parallel", "parallel", "arbitrary"),
        name="hgrn_prompt",
    )(p3, p3, p3, p3, lb.reshape(1, -1), gn.reshape(1, dh), s0)


def _hgrn_sample_kernel(xq_ref, xf_ref, xi_ref, xg_ref, lb_ref, gn_ref, s0_ref, o_ref, s_ref):
    n_new, dh = xq_ref.shape
    xq = xq_ref[...]
    f = lb_ref[...] + (1.0 - lb_ref[...]) * _sigmoid(xf_ref[...])
    q = xq * _sigmoid(xq)
    pad = jnp.zeros((8 - n_new % 8, dh), F32)
    cols = lambda a: jnp.concatenate([a, pad], axis=0).T
    f_c, k_c, q_c = cols(f), cols(1.0 - f), cols(q)
    v = xi_ref[...]
    s = s0_ref[...]
    outs = []
    for t in range(n_new):
        s = f_c[:, t:t + 1] * s + k_c[:, t:t + 1] * v[t:t + 1, :]
        outs.append(jnp.sum(q_c[:, t:t + 1] * s, axis=0, keepdims=True))
    o = jnp.concatenate(outs, axis=0)
    y = o * lax.rsqrt(jnp.mean(o * o, axis=-1, keepdims=True) + EPS) * gn_ref[...]
    o_ref[...] = y * _sigmoid(xg_ref[...])
    s_ref[...] = s


def _hgrn_sample(p3, lb, gn, state_all, layer, n_heads, col0):
    b, n_new, _ = p3.shape
    dh = HEAD_DIM
    c0 = col0 // dh
    spec = lambda off: pl.BlockSpec((None, n_new, dh), lambda bb, h: (bb, 0, c0 + off * n_heads + h))
    return pl.pallas_call(
        _hgrn_sample_kernel,
        grid=(b, n_heads),
        in_specs=[spec(0), spec(1), spec(2), spec(3),
                  pl.BlockSpec((1, dh), lambda bb, h: (0, h)), pl.BlockSpec((1, dh), lambda bb, h: (0, 0)),
                  pl.BlockSpec((None, None, None, dh, dh), lambda bb, h: (layer, bb, h, 0, 0))],
        out_specs=[pl.BlockSpec((None, n_new, dh), lambda bb, h: (bb, 0, h)),
                   pl.BlockSpec((None, None, dh, dh), lambda bb, h: (bb, h, 0, 0))],
        out_shape=[jax.ShapeDtypeStruct((b, n_new, n_heads * dh), F32),
                   jax.ShapeDtypeStruct(state_all.shape[1:], F32)],
        compiler_params=_params("parallel", "parallel"),
        name="hgrn_sample",
    )(p3, p3, p3, p3, lb.reshape(1, -1), gn.reshape(1, dh), state_all)


def _mem_attn_kernel(q_ref, k_ref, v_ref, g_ref, o_ref, *, scale):
    dm = g_ref.shape[1]
    for h in range(q_ref.shape[1] // dm):
        cols = slice(h * dm, (h + 1) * dm)
        q = q_ref[:, cols]
        qn = q * lax.rsqrt(jnp.mean(q * q, axis=-1, keepdims=True) + EPS) * g_ref[...]
        s = lax.dot_general(qn.astype(BF16), k_ref[:, cols].astype(BF16), _NT, preferred_element_type=F32) * scale
        p = jnp.exp(s - jnp.max(s, axis=-1, keepdims=True))
        o = jnp.dot(p.astype(BF16), v_ref[:, cols].astype(BF16), preferred_element_type=F32)
        o_ref[:, cols] = (o / jnp.sum(p, axis=-1, keepdims=True)).astype(o_ref.dtype)


def _mem_attn(p3, mem_k, mem_v, layer, gn, col0, tq, out_dtype):
    b, l, _ = p3.shape
    mtok, width = mem_k.shape[2], mem_k.shape[3]
    dm = width // MEM_HEADS
    kern = functools.partial(_mem_attn_kernel, scale=dm ** -0.5)
    kv = pl.BlockSpec((None, None, mtok, width), lambda bb, i: (layer, bb, 0, 0))
    return pl.pallas_call(
        kern,
        grid=(b, l // tq),
        in_specs=[pl.BlockSpec((None, tq, width), lambda bb, i: (bb, i, col0 // width)), kv, kv,
                  pl.BlockSpec((1, dm), lambda bb, i: (0, 0))],
        out_specs=pl.BlockSpec((None, tq, width), lambda bb, i: (bb, i, 0)),
        out_shape=jax.ShapeDtypeStruct((b, l, width), out_dtype),
        compiler_params=_params("parallel", "arbitrary"),
        name="mem_attn",
    )(p3, mem_k, mem_v, gn.reshape(1, dm))


def _merge_kernel(oa_ref, oh_ref, om_ref, ga_ref, gh_ref, gm_ref, wa_ref, wh_ref, wm_ref, o_ref):
    acc = _sigmoid(ga_ref[...]) * jnp.dot(oa_ref[...].astype(BF16), wa_ref[...], preferred_element_type=F32)
    acc = acc + _sigmoid(gh_ref[...]) * jnp.dot(oh_ref[...].astype(BF16), wh_ref[...], preferred_element_type=F32)
    acc = acc + _sigmoid(gm_ref[...]) * jnp.dot(om_ref[...].astype(BF16), wm_ref[...], preferred_element_type=F32)
    o_ref[...] = acc.astype(o_ref.dtype)


def _merge(o_a, o_h, o_m, p2, gate_col0, w_branch, layer, tm, tn):
    n, bw = o_a.shape
    d = w_branch.shape[3]
    gspec = lambda br: pl.BlockSpec((tm, tn), lambda i, j: (i, (gate_col0 + br * d) // tn + j))
    wspec = lambda br: pl.BlockSpec((None, None, bw, tn), lambda i, j: (layer, br, 0, j))
    ospec = pl.BlockSpec((tm, bw), lambda i, j: (i, 0))
    return pl.pallas_call(
        _merge_kernel,
        grid=(n // tm, d // tn),
        in_specs=[ospec, ospec, ospec, gspec(0), gspec(1), gspec(2), wspec(0), wspec(1), wspec(2)],
        out_specs=pl.BlockSpec((tm, tn), lambda i, j: (i, j)),
        out_shape=jax.ShapeDtypeStruct((n, d), BF16),
        compiler_params=_params("parallel", "arbitrary"),
        name="merge",
    )(o_a, o_h, o_m, p2, p2, p2, w_branch, w_branch, w_branch)


def _out_proj_kernel(m_ref, w_ref, x_ref, g_ref, y_ref, h_ref):
    y = x_ref[...] + jnp.dot(m_ref[...], w_ref[...], preferred_element_type=F32)
    y_ref[...] = y
    hn = y * lax.rsqrt(jnp.mean(y * y, axis=-1, keepdims=True) + EPS)
    h_ref[...] = (hn * g_ref[...]).astype(h_ref.dtype)


def _out_proj(m, w_out, layer, x, g, tm):
    n, d = x.shape
    row = pl.BlockSpec((tm, d), lambda i: (i, 0))
    return pl.pallas_call(
        _out_proj_kernel,
        grid=(n // tm,),
        in_specs=[row, pl.BlockSpec((None, d, d), lambda i: (layer, 0, 0)), row, pl.BlockSpec((1, d), lambda i: (0, 0))],
        out_specs=[row, row],
        out_shape=[jax.ShapeDtypeStruct((n, d), F32), jax.ShapeDtypeStruct((n, d), BF16)],
        compiler_params=_params("parallel"),
        name="out_proj",
    )(m, w_out, x, g.reshape(1, d))


def _mlp_kernel(h_ref, wu_ref, wd_ref, x_ref, g_ref, y_ref, hn_ref, acc):
    f = pl.program_id(1)
    u = jnp.maximum(jnp.dot(h_ref[...], wu_ref[...], preferred_element_type=F32), 0.0)
    part = jnp.dot((u * u).astype(BF16), wd_ref[...], preferred_element_type=F32)

    @pl.when(f == 0)
    def _():
        acc[...] = part

    @pl.when(f > 0)
    def _():
        acc[...] = acc[...] + part

    @pl.when(f == pl.num_programs(1) - 1)
    def _():
        y = x_ref[...] + acc[...]
        y_ref[...] = y
        hn = y * lax.rsqrt(jnp.mean(y * y, axis=-1, keepdims=True) + EPS)
        hn_ref[...] = (hn * g_ref[...]).astype(hn_ref.dtype)


def _mlp(h, w_up, w_down, layer, x, g_next, tm, tf):
    n, d = x.shape
    dff = w_up.shape[2]
    row = pl.BlockSpec((tm, d), lambda i, f: (i, 0))
    return pl.pallas_call(
        _mlp_kernel,
        grid=(n // tm, dff // tf),
        in_specs=[row, pl.BlockSpec((None, d, tf), lambda i, f: (layer, 0, f)),
                  pl.BlockSpec((None, tf, d), lambda i, f: (layer, f, 0)), row,
                  pl.BlockSpec((1, d), lambda i, f: (0, 0))],
        out_specs=[row, row],
        out_shape=[jax.ShapeDtypeStruct((n, d), F32), jax.ShapeDtypeStruct((n, d), BF16)],
        scratch_shapes=[pltpu.VMEM((tm, d), F32)],
        compiler_params=_params("parallel", "arbitrary"),
        name="mlp",
    )(h, w_up, w_down, x, g_next.reshape(1, d))


def _tiles(n):
    return (min(n, 1024), min(n, 512), min(n, 256), min(n, 512))


def _trunk_layer(layer, x, h, shape, w, lb, next_gain, attn_fn, hgrn_fn, mem_fn):
    b, l = shape
    n, d = x.shape
    t_proj, t_merge, t_out, t_mlp = _tiles(n)
    n_heads = w["hg_width"] // HEAD_DIM
    qk_gain = jnp.concatenate([jnp.tile(w["q_norm_moba"][layer], n_heads), jnp.tile(w["k_norm_moba"][layer], n_heads)])
    p2 = _proj(h, w["w_in"], layer, qk_gain.reshape(1, -1), HEAD_DIM, t_proj, 1024)
    p3 = p2.reshape(b, l, -1)
    o_a = attn_fn(p3)
    o_h, s_new = hgrn_fn(p3, lb)
    o_m = mem_fn(p3)
    flat = lambda a: a.reshape(n, -1)
    m = _merge(flat(o_a), flat(o_h), flat(o_m), p2, w["gate_col0"], w["w_branch"], layer, t_merge, 512)
    x, h2 = _out_proj(m, w["w_out"], layer, x, w["ln_mlp"][layer], t_out)
    x, h_next = _mlp(h2, w["w_up"], w["w_down"], layer, x, next_gain, t_mlp, 1024)
    return x, h_next, p3, s_new


def kernel(x_prompt, x_sample, mem_prompt, cache_moba_k, cache_moba_v, cache_mem_k, cache_mem_v, state_hgrn,
           page_table, ln_mix, w_in, q_norm_moba, k_norm_moba, lb_logits, hg_out_norm, q_norm_mem, ln_mem,
           w_mem_kv, k_norm_mem, w_branch, w_out, ln_mlp, w_up, w_down):
    depth = w_in.shape[0]
    bp, lp, d = x_prompt.shape
    bs, ls, _ = x_sample.shape
    hg_width = lb_logits.shape[1]
    n_heads = hg_width // HEAD_DIM
    moba_width = n_heads * HEAD_DIM
    mem_width = cache_mem_k.shape[3] * cache_mem_k.shape[4]
    mem_tokens = mem_prompt.shape[1]
    hg_col0 = 3 * moba_width
    mem_col0 = hg_col0 + 4 * hg_width
    gate_col0 = mem_col0 + mem_width

    w = dict(
        w_in=w_in.astype(BF16), w_branch=w_branch.astype(BF16), w_out=w_out.astype(BF16),
        w_up=w_up.astype(BF16), w_down=w_down.astype(BF16), ln_mlp=ln_mlp,
        q_norm_moba=q_norm_moba, k_norm_moba=k_norm_moba, hg_width=hg_width, gate_col0=gate_col0,
    )
    w_mem = w_mem_kv.astype(BF16)
    lower = _lower_bounds(lb_logits)

    ck = cache_moba_k.reshape(cache_moba_k.shape[:3] + (moba_width,))
    cv = cache_moba_v.reshape(cache_moba_v.shape[:3] + (moba_width,))
    cmk = cache_mem_k.reshape(cache_mem_k.shape[:3] + (mem_width,))
    cmv = cache_mem_v.reshape(cache_mem_v.shape[:3] + (mem_width,))

    xp = x_prompt.reshape(bp * lp, d)
    xs = x_sample.reshape(bs * ls, d)
    hp = _rmsnorm_bf16(xp, ln_mix[0], min(bp * lp, 512))
    hs = _rmsnorm_bf16(xs, ln_mix[0], bs * ls)
    s0_p = jnp.zeros((bp,) + state_hgrn.shape[2:], F32)
    mem_flat = mem_prompt.reshape(bp * mem_tokens, d)

    kp_l, vp_l, mkp_l, mvp_l, sp_l, ks_l, vs_l, ss_l = [], [], [], [], [], [], [], []
    for layer in range(depth):
        next_gain = ln_mix[layer + 1] if layer + 1 < depth else ln_mix[layer]
        gn_h = hg_out_norm[layer]
        gn_m = q_norm_mem[layer]

        hm = _rmsnorm_bf16(mem_flat, ln_mem[layer], min(bp * mem_tokens, 512))
        mem_gain = jnp.tile(k_norm_mem[layer], MEM_HEADS).reshape(1, -1)
        mkv = _proj(hm, w_mem, layer, mem_gain, mem_width // MEM_HEADS, min(bp * mem_tokens, 512), 1024)
        mk = mkv[:, :mem_width].reshape(1, bp, mem_tokens, mem_width)
        mv = mkv[:, mem_width:].reshape(1, bp, mem_tokens, mem_width)

        xp, hp, pp, sp = _trunk_layer(
            layer, xp, hp, (bp, lp), w, lower[layer], next_gain,
            lambda p3: _moba_prompt(p3, n_heads),
            lambda p3, lb: _hgrn_prompt(p3, lb, gn_h, s0_p, n_heads, hg_col0),
            lambda p3: _mem_attn(p3, mk, mv, 0, gn_m, mem_col0, min(lp, 512), BF16))
        xs, hs, ps, ss = _trunk_layer(
            layer, xs, hs, (bs, ls), w, lower[layer], next_gain,
            lambda p3: _moba_sample(p3, ck, cv, layer, page_table, n_heads),
            lambda p3, lb: _hgrn_sample(p3, lb, gn_h, state_hgrn, layer, n_heads, hg_col0),
            lambda p3: _mem_attn(p3, cmk, cmv, layer, gn_m, mem_col0, ls, F32))

        kv_shape = lambda p3: p3.shape[:2] + (n_heads, HEAD_DIM)
        kp_l.append(pp[:, :, moba_width:2 * moba_width].reshape(kv_shape(pp)))
        vp_l.append(pp[:, :, 2 * moba_width:3 * moba_width].reshape(kv_shape(pp)))
        ks_l.append(ps[:, :, moba_width:2 * moba_width].reshape(kv_shape(ps)))
        vs_l.append(ps[:, :, 2 * moba_width:3 * moba_width].reshape(kv_shape(ps)))
        mem_shape = (bp, mem_tokens, MEM_HEADS, mem_width // MEM_HEADS)
        mkp_l.append(mk.reshape(mem_shape))
        mvp_l.append(mv.reshape(mem_shape))
        sp_l.append(sp)
        ss_l.append(ss)

    return (xp.reshape(bp, lp, d), xs.reshape(bs, ls, d), jnp.stack(kp_l), jnp.stack(vp_l), jnp.stack(mkp_l),
            jnp.stack(mvp_l), jnp.stack(sp_l), jnp.stack(ks_l), jnp.stack(vs_l), jnp.stack(ss_l))
```

```python
import functools

import jax
import jax.numpy as jnp
from jax import lax
from jax.experimental import pallas as pl
from jax.experimental.pallas import tpu as pltpu

F32 = jnp.float32
BF16 = jnp.bfloat16

EPS = 1e-6
MASK_FILL = -1e30
MOBA_BLOCK = 256
MOBA_TOPK = 3
HEAD_DIM = 128
MEM_HEADS = 4
MOBA_QB = 2
LOG2E = 1.4426950408889634
HG_CHUNK = 64
HG_SUB = 8
HG_STEP = 256
PAGES_PER_STEP = 8
VMEM_LIMIT_BYTES = 56 * 1024 * 1024

_NT = (((1,), (1,)), ((), ()))
_TN = (((0,), (0,)), ((), ()))


def _params(*sem):
    return pltpu.CompilerParams(dimension_semantics=sem, vmem_limit_bytes=VMEM_LIMIT_BYTES)


def _sigmoid(x):
    return 1.0 / (1.0 + jnp.exp(-x))


def _lower_bounds_kernel(lg_ref, out_ref):
    x = lg_ref[...]
    e = jnp.exp(x - jnp.max(x, axis=0, keepdims=True))
    p = e / jnp.sum(e, axis=0, keepdims=True)
    c = p[0:1]
    rows = [c - p[0:1]]
    for l in range(1, x.shape[0]):
        c = c + p[l:l + 1]
        rows.append(c - p[0:1])
    out_ref[...] = jnp.concatenate(rows, axis=0)


def _lower_bounds(lb_logits):
    return pl.pallas_call(
        _lower_bounds_kernel,
        out_shape=jax.ShapeDtypeStruct(lb_logits.shape, F32),
        name="lower_bounds",
    )(lb_logits)


def _rmsnorm_kernel(x_ref, g_ref, o_ref):
    x = x_ref[...]
    y = x * lax.rsqrt(jnp.mean(x * x, axis=-1, keepdims=True) + EPS)
    o_ref[...] = (y * g_ref[...]).astype(o_ref.dtype)


def _rmsnorm_bf16(x, g, tm):
    n, d = x.shape
    return pl.pallas_call(
        _rmsnorm_kernel,
        grid=(n // tm,),
        in_specs=[pl.BlockSpec((tm, d), lambda i: (i, 0)), pl.BlockSpec((1, d), lambda i: (0, 0))],
        out_specs=pl.BlockSpec((tm, d), lambda i: (i, 0)),
        out_shape=jax.ShapeDtypeStruct((n, d), BF16),
        compiler_params=_params("parallel"),
        name="rmsnorm",
    )(x, g.reshape(1, d))


def _proj_kernel(h_ref, w_ref, g_ref, o_ref, *, n_norm_tiles, group):
    acc = jnp.dot(h_ref[...], w_ref[...], preferred_element_type=F32)
    j = pl.program_id(1)

    @pl.when(j < n_norm_tiles)
    def _():
        for c in range(acc.shape[1] // group):
            a = acc[:, c * group:(c + 1) * group]
            y = a * lax.rsqrt(jnp.mean(a * a, axis=-1, keepdims=True) + EPS)
            o_ref[:, c * group:(c + 1) * group] = y * g_ref[:, c * group:(c + 1) * group]

    @pl.when(j >= n_norm_tiles)
    def _():
        o_ref[...] = acc


def _proj(h, w_all, layer, gains, group, tm, tn):
    n, k = h.shape
    m = w_all.shape[2]
    n_norm_tiles = gains.shape[1] // tn
    kern = functools.partial(_proj_kernel, n_norm_tiles=n_norm_tiles, group=group)
    return pl.pallas_call(
        kern,
        grid=(n // tm, m // tn),
        in_specs=[
            pl.BlockSpec((tm, k), lambda i, j: (i, 0)),
            pl.BlockSpec((None, k, tn), lambda i, j: (layer, 0, j)),
            pl.BlockSpec((1, tn), lambda i, j: (0, jnp.minimum(j, n_norm_tiles - 1))),
        ],
        out_specs=pl.BlockSpec((tm, tn), lambda i, j: (i, j)),
        out_shape=jax.ShapeDtypeStruct((n, m), F32),
        compiler_params=_params("parallel", "arbitrary"),
        name="proj",
    )(h, w_all, gains)


def _moba_prompt_kernel(q_ref, k_ref, v_ref, o_ref, kbf, vt, kmean, selbias, s_scr, acc_scr, *, nblk, k_sel, scale):
    t = pl.program_id(2)
    blk = MOBA_BLOCK
    tq = MOBA_QB * blk

    @pl.when(t == 0)
    def _():
        for j in range(nblk):
            kj = k_ref[j * blk:(j + 1) * blk, :]
            kbf[j * blk:(j + 1) * blk, :] = kj.astype(BF16)
            kmean[j:j + 1, :] = jnp.mean(kj, axis=0, keepdims=True)
        for jp in range(nblk // MOBA_QB):
            vt[jp] = v_ref[jp * tq:(jp + 1) * tq, :].T.astype(BF16)

    q = q_ref[...]
    qb = (q * (scale * LOG2E)).astype(BF16)

    sc = lax.dot_general(kmean[...], q, _NT, precision=lax.Precision.HIGHEST,
                         preferred_element_type=F32)
    jj = lax.broadcasted_iota(jnp.int32, sc.shape, 0)
    own = MOBA_QB * t + lax.broadcasted_iota(jnp.int32, sc.shape, 1) // blk
    past = jj < own
    sc = jnp.where(past, sc, MASK_FILL)
    rank = jnp.zeros(sc.shape, jnp.int32)
    for j2 in range(nblk):
        sj = sc[j2:j2 + 1, :]
        beats = (sj > sc) | ((sj == sc) & (j2 < jj))
        rank = rank + beats.astype(jnp.int32)
    selbias[...] = jnp.where(past & (rank < k_sel), 0.0, MASK_FILL)

    def logits(jp):
        kj = kbf[pl.ds(pl.multiple_of(jp * tq, tq), tq), :]
        return lax.dot_general(kj, qb, _NT, preferred_element_type=F32)

    def bias(jp):
        rows = [jnp.broadcast_to(selbias[pl.ds(jp * MOBA_QB + u, 1), :], (blk, tq)) for u in range(MOBA_QB)]
        return jnp.concatenate(rows, axis=0)

    def past_group(jp, m):
        s = logits(jp) + bias(jp)
        s_scr[jp] = s
        return jnp.maximum(m, jnp.max(s, axis=0, keepdims=True))

    m = lax.fori_loop(0, t, past_group, jnp.full((1, tq), MASK_FILL, F32))

    s = logits(t)
    kpos = lax.broadcasted_iota(jnp.int32, s.shape, 0)
    qpos = lax.broadcasted_iota(jnp.int32, s.shape, 1)
    s = jnp.where((kpos // blk == qpos // blk) & (kpos <= qpos), s, s + bias(t))
    s_scr[t] = s
    m = jnp.maximum(m, jnp.max(s, axis=0, keepdims=True))

    acc_scr[...] = jnp.zeros(acc_scr.shape, F32)

    def accumulate(jp, l):
        p = jnp.exp2(s_scr[jp] - m)
        acc_scr[...] += jnp.dot(vt[jp], p.astype(BF16), preferred_element_type=F32)
        return l + jnp.sum(p, axis=0, keepdims=True)

    l = lax.fori_loop(0, t + 1, accumulate, jnp.zeros((1, tq), F32))
    o_ref[...] = (acc_scr[...] / l).T.astype(o_ref.dtype)


def _moba_prompt(p3, n_heads):
    b, l, _ = p3.shape
    tq = MOBA_QB * MOBA_BLOCK
    assert l % tq == 0
    nblk = l // MOBA_BLOCK
    dh = HEAD_DIM
    kern = functools.partial(_moba_prompt_kernel, nblk=nblk, k_sel=min(MOBA_TOPK, nblk - 1), scale=dh ** -0.5)
    return pl.pallas_call(
        kern,
        grid=(b, n_heads, l // tq),
        in_specs=[
            pl.BlockSpec((None, tq, dh), lambda bb, h, i: (bb, i, h)),
            pl.BlockSpec((None, l, dh), lambda bb, h, i: (bb, 0, n_heads + h)),
            pl.BlockSpec((None, l, dh), lambda bb, h, i: (bb, 0, 2 * n_heads + h)),
        ],
        out_specs=pl.BlockSpec((None, tq, dh), lambda bb, h, i: (bb, i, h)),
        out_shape=jax.ShapeDtypeStruct((b, l, n_heads * dh), BF16),
        scratch_shapes=[
            pltpu.VMEM((l, dh), BF16),
            pltpu.VMEM((l // tq, dh, tq), BF16),
            pltpu.VMEM((nblk, dh), F32),
            pltpu.VMEM((nblk, tq), F32),
            pltpu.VMEM((l // tq, tq, tq), F32),
            pltpu.VMEM((dh, tq), F32),
        ],
        compiler_params=_params("parallel", "parallel", "arbitrary"),
        name="moba_prompt",
    )(p3, p3, p3)


def _moba_sample_kernel(pt_ref, q_ref, kn_ref, vn_ref, *rest, n_steps, n_heads, n_new, page, k_sel, scale):
    g = PAGES_PER_STEP
    k_refs = rest[:g]
    v_refs = rest[g:2 * g]
    o_ref = rest[2 * g]
    s_all, ksum, selb, headb, newpad, acc, lacc, m_scr, lnew = rest[2 * g + 1:]
    ph = pl.program_id(1)
    st = pl.program_id(2)
    dh = HEAD_DIM
    rows = n_new * n_heads
    lanes = page * n_heads
    ppb = MOBA_BLOCK // page
    n_pages = n_steps * g
    n_past = n_pages // ppb
    vw = selb.shape[2]

    def same_head(shape):
        r = lax.broadcasted_iota(jnp.int32, shape, 0)
        c = lax.broadcasted_iota(jnp.int32, shape, 1)
        return (r // n_new) == (c % n_heads)

    def qscaled():
        return (q_ref[...] * (scale * LOG2E)).astype(BF16)

    def page_bias(pidx):
        return jnp.concatenate([selb[pidx // ppb]] * (lanes // vw), axis=1)

    @pl.when(ph == 0)
    def _():
        qb16 = qscaled()
        for u in range(g):
            kp = k_refs[u][...]
            s_all[st * g + u] = lax.dot_general(qb16, kp.astype(BF16), _NT, preferred_element_type=F32)
            srow = jnp.sum(kp.reshape(page, n_heads, dh), axis=0)
            dst = pl.ds(pl.multiple_of((st * (g // ppb) + u // ppb) * n_heads, n_heads), n_heads)
            if u % ppb == 0:
                ksum[dst, :] = srow
            else:
                ksum[dst, :] = ksum[dst, :] + srow

    @pl.when((ph == 1) & (st == 0))
    def _():
        kmean = ksum[...] * (1.0 / MOBA_BLOCK)
        sc2 = lax.dot_general(q_ref[...], kmean, _NT, precision=lax.Precision.HIGHEST,
                              preferred_element_type=F32)
        sc2 = jnp.where(same_head(sc2.shape), sc2, 0.0)
        gr = lax.broadcasted_iota(jnp.int32, (n_past * n_heads, n_past), 0)
        gc = lax.broadcasted_iota(jnp.int32, (n_past * n_heads, n_past), 1)
        pick = jnp.where(gr // n_heads == gc, 1.0, 0.0)
        sc = jnp.dot(sc2, pick, precision=lax.Precision.HIGHEST, preferred_element_type=F32)
        jj = lax.broadcasted_iota(jnp.int32, sc.shape, 1)
        rank = jnp.zeros(sc.shape, jnp.int32)
        for j2 in range(n_past):
            sj = sc[:, j2:j2 + 1]
            beats = (sj > sc) | ((sj == sc) & (j2 < jj))
            rank = rank + beats.astype(jnp.int32)
        sel = jnp.where(rank < k_sel, 0.0, MASK_FILL)
        for j2 in range(n_past):
            selb[j2] = jnp.broadcast_to(sel[:, j2:j2 + 1], (rows, vw))
        headb[...] = jnp.where(same_head((rows, lanes)), 0.0, MASK_FILL)

        qb16 = qscaled()
        newpad[...] = jnp.zeros(newpad.shape, F32)
        newpad[0:rows, :] = kn_ref[...]
        s_new = lax.dot_general(qb16, newpad[...].astype(BF16), _NT, preferred_element_type=F32)
        r = lax.broadcasted_iota(jnp.int32, s_new.shape, 0)
        c = lax.broadcasted_iota(jnp.int32, s_new.shape, 1)
        s_new = jnp.where(same_head(s_new.shape) & (c // n_heads <= r % n_new), s_new, MASK_FILL)

        def running_max(pidx, mx):
            return jnp.maximum(mx, s_all[pidx] + page_bias(pidx))

        mx = lax.fori_loop(0, n_pages, running_max, jnp.full((rows, lanes), 2 * MASK_FILL, F32))
        m = jnp.maximum(jnp.max(mx + headb[...], axis=-1, keepdims=True), jnp.max(s_new, axis=-1, keepdims=True))
        m_scr[...] = m
        p_new = jnp.exp2(s_new - m)
        lnew[...] = jnp.sum(p_new, axis=-1, keepdims=True)
        newpad[...] = jnp.zeros(newpad.shape, F32)
        newpad[0:rows, :] = vn_ref[...]
        acc[...] = jnp.dot(p_new.astype(BF16), newpad[...].astype(BF16), preferred_element_type=F32)
        lacc[...] = jnp.zeros(lacc.shape, F32)

    @pl.when(ph == 1)
    def _():
        a = acc[...]
        ls = lacc[...]
        off = headb[...] - m_scr[...]
        for u in range(g):
            pidx = st * g + u
            p = jnp.exp2(s_all[pidx] + page_bias(pidx) + off)
            ls = ls + p
            a = a + jnp.dot(p.astype(BF16), v_refs[u][...].astype(BF16), preferred_element_type=F32)
        acc[...] = a
        lacc[...] = ls

    @pl.when((ph == 1) & (st == n_steps - 1))
    def _():
        o_ref[...] = acc[...] / (jnp.sum(lacc[...], axis=-1, keepdims=True) + lnew[...])


def _moba_sample(p3, cache_k, cache_v, layer, page_table, n_heads):
    b, n_new, _ = p3.shape
    dh = HEAD_DIM
    width = n_heads * dh
    lanes = cache_k.shape[2]
    page = lanes // n_heads
    n_pages = page_table.shape[1]
    g = PAGES_PER_STEP
    ppb = MOBA_BLOCK // page
    rows = n_new * n_heads
    vw = 128
    assert MOBA_BLOCK % page == 0 and g % ppb == 0 and n_pages % g == 0 and rows <= vw and lanes % vw == 0
    n_steps = n_pages // g
    n_past = n_pages // ppb
    kern = functools.partial(_moba_sample_kernel, n_steps=n_steps, n_heads=n_heads, n_new=n_new, page=page,
                             k_sel=min(MOBA_TOPK, n_past), scale=dh ** -0.5)

    heads = lambda c0: p3[:, :, c0:c0 + width].reshape(b, n_new, n_heads, dh)
    q_rows = heads(0).transpose(0, 2, 1, 3).reshape(b, rows, dh)
    kn_rows = heads(width).reshape(b, rows, dh)
    vn_rows = heads(2 * width).reshape(b, rows, dh)

    def k_map(u):
        return lambda bb, ph, st, pt: (layer, pt[bb, jnp.where(ph == 0, st, n_steps - 1) * g + u], 0, 0)

    def v_map(u):
        return lambda bb, ph, st, pt: (layer, pt[bb, jnp.where(ph == 0, 0, st) * g + u], 0, 0)

    new_spec = pl.BlockSpec((None, rows, dh), lambda bb, ph, st, pt: (bb, 0, 0))
    page_spec = lambda imap: pl.BlockSpec((None, None, lanes, dh), imap)
    grid_spec = pltpu.PrefetchScalarGridSpec(
        num_scalar_prefetch=1,
        grid=(b, 2, n_steps),
        in_specs=[new_spec, new_spec, new_spec]
                 + [page_spec(k_map(u)) for u in range(g)] + [page_spec(v_map(u)) for u in range(g)],
        out_specs=new_spec,
        scratch_shapes=[
            pltpu.VMEM((n_pages, rows, lanes), F32),
            pltpu.VMEM((n_past * n_heads, dh), F32),
            pltpu.VMEM((n_past, rows, vw), F32),
            pltpu.VMEM((rows, lanes), F32),
            pltpu.VMEM((vw, dh), F32),
            pltpu.VMEM((rows, dh), F32),
            pltpu.VMEM((rows, lanes), F32),
            pltpu.VMEM((rows, 1), F32),
            pltpu.VMEM((rows, 1), F32),
        ],
    )
    o = pl.pallas_call(
        kern,
        grid_spec=grid_spec,
        out_shape=jax.ShapeDtypeStruct((b, rows, dh), F32),
        compiler_params=_params("parallel", "arbitrary", "arbitrary"),
        name="moba_sample",
    )(page_table, q_rows, kn_rows, vn_rows, *([cache_k] * g), *([cache_v] * g))
    return o.reshape(b, n_heads, n_new, dh).transpose(0, 2, 1, 3).reshape(b, n_new, width)


def _hgrn_levels():
    sizes, h = [], HG_SUB
    while h < HG_CHUNK:
        sizes.append(h)
        h *= 2
    return sizes


def _hgrn_prompt_kernel(xq_ref, xf_ref, xi_ref, xg_ref, lb_ref, gn_ref, s0_ref, o_ref, s_ref,
                        st_scr, tril_scr, group_scr):
    t = pl.program_id(2)
    n, dh = xq_ref.shape
    c, sub = HG_CHUNK, HG_SUB
    levels = _hgrn_levels()

    @pl.when(t == 0)
    def _():
        st_scr[...] = s0_ref[...].T
        r = lax.broadcasted_iota(jnp.int32, (n, n), 0)
        cc = lax.broadcasted_iota(jnp.int32, (n, n), 1)
        tril_scr[...] = jnp.where((cc <= r) & (r // c == cc // c), 1.0, 0.0)
        for li, hs in enumerate(levels):
            group_scr[li] = jnp.where(r // (2 * hs) == cc // (2 * hs), 1.0, 0.0)

    lb = lb_ref[...]
    xq = xq_ref[...]
    f = lb + (1.0 - lb) * _sigmoid(xf_ref[...])
    k = 1.0 - f
    q = xq * _sigmoid(xq)
    v = xi_ref[...]
    vb = v.astype(BF16)
    b = jnp.dot(tril_scr[...], jnp.log2(f), precision=lax.Precision.HIGHEST, preferred_element_type=F32)

    a = None
    for li, hs in enumerate(levels):
        zero = jnp.zeros((hs, dh), F32)
        q_rows, k_rows = [], []
        for g0 in range(0, n, 2 * hs):
            first, second = slice(g0, g0 + hs), slice(g0 + hs, g0 + 2 * hs)
            bref = b[g0 + hs - 1:g0 + hs, :]
            q_rows += [zero, q[second] * jnp.exp2(b[second] - bref)]
            k_rows += [k[first] * jnp.exp2(bref - b[first]), zero]
        ql = jnp.concatenate(q_rows, axis=0).astype(BF16)
        kl = jnp.concatenate(k_rows, axis=0).astype(BF16)
        al = lax.dot_general(ql, kl, _NT, preferred_element_type=F32) * group_scr[li]
        a = al if a is None else a + al
    o = jnp.dot(a.astype(BF16), vb, preferred_element_type=F32)

    q3, k3, b3, v3 = (x.reshape(n // sub, sub, dh) for x in (q, k, b, v))
    tio = lax.broadcasted_iota(jnp.int32, q3.shape, 1)
    od = jnp.zeros(q3.shape, F32)
    for s in range(sub):
        e = jnp.exp2(jnp.where(tio >= s, b3 - b3[:, s:s + 1, :], MASK_FILL))
        w = q3 * k3[:, s:s + 1, :] * e
        od = od + jnp.sum(w, axis=-1, keepdims=True) * v3[:, s:s + 1, :]
    o = o + od.reshape(n, dh)

    st = st_scr[...]
    parts = []
    for c0 in range(0, n, c):
        rows = slice(c0, c0 + c)
        bc = b[rows]
        b_end = bc[c - 1:c, :]
        qd = (q[rows] * jnp.exp2(bc)).astype(BF16)
        parts.append(o[rows] + lax.dot_general(qd, st.astype(BF16), _NT, preferred_element_type=F32))
        kd = (k[rows] * jnp.exp2(b_end - bc)).astype(BF16)
        st = st * jnp.exp2(b_end) + lax.dot_general(vb[rows], kd, _TN, preferred_element_type=F32)
    st_scr[...] = st
    o = jnp.concatenate(parts, axis=0)

    y = o * lax.rsqrt(jnp.mean(o * o, axis=-1, keepdims=True) + EPS) * gn_ref[...]
    o_ref[...] = (y * _sigmoid(xg_ref[...])).astype(o_ref.dtype)

    @pl.when(t == pl.num_programs(2) - 1)
    def _():
        s_ref[...] = st.T


def _hgrn_prompt(p3, lb, gn, s0, n_heads, col0):
    b, l, _ = p3.shape
    dh = HEAD_DIM
    step = min(HG_STEP, l)
    assert l % step == 0 and step % HG_CHUNK == 0
    c0 = col0 // dh
    spec = lambda off: pl.BlockSpec((None, step, dh), lambda bb, h, t: (bb, t, c0 + off * n_heads + h))
    vec = pl.BlockSpec((1, dh), lambda bb, h, t: (0, h))
    state = pl.BlockSpec((None, None, dh, dh), lambda bb, h, t: (bb, h, 0, 0))
    return pl.pallas_call(
        _hgrn_prompt_kernel,
        grid=(b, n_heads, l // step),
        in_specs=[spec(0), spec(1), spec(2), spec(3), vec, pl.BlockSpec((1, dh), lambda bb, h, t: (0, 0)), state],
        out_specs=[pl.BlockSpec((None, step, dh), lambda bb, h, t: (bb, t, h)), state],
        out_shape=[jax.ShapeDtypeStruct((b, l, n_heads * dh), BF16), jax.ShapeDtypeStruct(s0.shape, F32)],
        scratch_shapes=[pltpu.VMEM((dh, dh), F32), pltpu.VMEM((step, step), F32),
                        pltpu.VMEM((len(_hgrn_levels()), step, step), F32)],
        compiler_params=_params("parallel", "parallel", "arbitrary"),
        name="hgrn_prompt",
    )(p3, p3, p3, p3, lb.reshape(1, -1), gn.reshape(1, dh), s0)


def _hgrn_sample_kernel(xq_ref, xf_ref, xi_ref, xg_ref, lb_ref, gn_ref, s0_ref, o_ref, s_ref):
    n_new, dh = xq_ref.shape
    xq = xq_ref[...]
    f = lb_ref[...] + (1.0 - lb_ref[...]) * _sigmoid(xf_ref[...])
    q = xq * _sigmoid(xq)
    pad = jnp.zeros((8 - n_new % 8, dh), F32)
    cols = lambda a: jnp.concatenate([a, pad], axis=0).T
    f_c, k_c, q_c = cols(f), cols(1.0 - f), cols(q)
    v = xi_ref[...]
    s = s0_ref[...]
    outs = []
    for t in range(n_new):
        s = f_c[:, t:t + 1] * s + k_c[:, t:t + 1] * v[t:t + 1, :]
        outs.append(jnp.sum(q_c[:, t:t + 1] * s, axis=0, keepdims=True))
    o = jnp.concatenate(outs, axis=0)
    y = o * lax.rsqrt(jnp.mean(o * o, axis=-1, keepdims=True) + EPS) * gn_ref[...]
    o_ref[...] = y * _sigmoid(xg_ref[...])
    s_ref[...] = s


def _hgrn_sample(p3, lb, gn, state_all, layer, n_heads, col0):
    b, n_new, _ = p3.shape
    dh = HEAD_DIM
    c0 = col0 // dh
    spec = lambda off: pl.BlockSpec((None, n_new, dh), lambda bb, h: (bb, 0, c0 + off * n_heads + h))
    return pl.pallas_call(
        _hgrn_sample_kernel,
        grid=(b, n_heads),
        in_specs=[spec(0), spec(1), spec(2), spec(3),
                  pl.BlockSpec((1, dh), lambda bb, h: (0, h)), pl.BlockSpec((1, dh), lambda bb, h: (0, 0)),
                  pl.BlockSpec((None, None, None, dh, dh), lambda bb, h: (layer, bb, h, 0, 0))],
        out_specs=[pl.BlockSpec((None, n_new, dh), lambda bb, h: (bb, 0, h)),
                   pl.BlockSpec((None, None, dh, dh), lambda bb, h: (bb, h, 0, 0))],
        out_shape=[jax.ShapeDtypeStruct((b, n_new, n_heads * dh), F32),
                   jax.ShapeDtypeStruct(state_all.shape[1:], F32)],
        compiler_params=_params("parallel", "parallel"),
        name="hgrn_sample",
    )(p3, p3, p3, p3, lb.reshape(1, -1), gn.reshape(1, dh), state_all)


def _mem_attn_kernel(q_ref, k_ref, v_ref, g_ref, o_ref, *, scale):
    dm = g_ref.shape[1]
    for h in range(q_ref.shape[1] // dm):
        cols = slice(h * dm, (h + 1) * dm)
        q = q_ref[:, cols]
        qn = q * lax.rsqrt(jnp.mean(q * q, axis=-1, keepdims=True) + EPS) * g_ref[...]
        s = lax.dot_general(qn.astype(BF16), k_ref[:, cols].astype(BF16), _NT, preferred_element_type=F32) * scale
        p = jnp.exp(s - jnp.max(s, axis=-1, keepdims=True))
        o = jnp.dot(p.astype(BF16), v_ref[:, cols].astype(BF16), preferred_element_type=F32)
        o_ref[:, cols] = (o / jnp.sum(p, axis=-1, keepdims=True)).astype(o_ref.dtype)


def _mem_attn(p3, mem_k, mem_v, layer, gn, col0, tq, out_dtype):
    b, l, _ = p3.shape
    mtok, width = mem_k.shape[2], mem_k.shape[3]
    dm = width // MEM_HEADS
    kern = functools.partial(_mem_attn_kernel, scale=dm ** -0.5)
    kv = pl.BlockSpec((None, None, mtok, width), lambda bb, i: (layer, bb, 0, 0))
    return pl.pallas_call(
        kern,
        grid=(b, l // tq),
        in_specs=[pl.BlockSpec((None, tq, width), lambda bb, i: (bb, i, col0 // width)), kv, kv,
                  pl.BlockSpec((1, dm), lambda bb, i: (0, 0))],
        out_specs=pl.BlockSpec((None, tq, width), lambda bb, i: (bb, i, 0)),
        out_shape=jax.ShapeDtypeStruct((b, l, width), out_dtype),
        compiler_params=_params("parallel", "arbitrary"),
        name="mem_attn",
    )(p3, mem_k, mem_v, gn.reshape(1, dm))


def _merge_kernel(oa_ref, oh_ref, om_ref, ga_ref, gh_ref, gm_ref, wa_ref, wh_ref, wm_ref, o_ref):
    acc = _sigmoid(ga_ref[...]) * jnp.dot(oa_ref[...].astype(BF16), wa_ref[...], preferred_element_type=F32)
    acc = acc + _sigmoid(gh_ref[...]) * jnp.dot(oh_ref[...].astype(BF16), wh_ref[...], preferred_element_type=F32)
    acc = acc + _sigmoid(gm_ref[...]) * jnp.dot(om_ref[...].astype(BF16), wm_ref[...], preferred_element_type=F32)
    o_ref[...] = acc.astype(o_ref.dtype)


def _merge(o_a, o_h, o_m, p2, gate_col0, w_branch, layer, tm, tn):
    n, bw = o_a.shape
    d = w_branch.shape[3]
    gspec = lambda br: pl.BlockSpec((tm, tn), lambda i, j: (i, (gate_col0 + br * d) // tn + j))
    wspec = lambda br: pl.BlockSpec((None, None, bw, tn), lambda i, j: (layer, br, 0, j))
    ospec = pl.BlockSpec((tm, bw), lambda i, j: (i, 0))
    return pl.pallas_call(
        _merge_kernel,
        grid=(n // tm, d // tn),
        in_specs=[ospec, ospec, ospec, gspec(0), gspec(1), gspec(2), wspec(0), wspec(1), wspec(2)],
        out_specs=pl.BlockSpec((tm, tn), lambda i, j: (i, j)),
        out_shape=jax.ShapeDtypeStruct((n, d), BF16),
        compiler_params=_params("parallel", "arbitrary"),
        name="merge",
    )(o_a, o_h, o_m, p2, p2, p2, w_branch, w_branch, w_branch)


def _out_proj_kernel(m_ref, w_ref, x_ref, g_ref, y_ref, h_ref):
    y = x_ref[...] + jnp.dot(m_ref[...], w_ref[...], preferred_element_type=F32)
    y_ref[...] = y
    hn = y * lax.rsqrt(jnp.mean(y * y, axis=-1, keepdims=True) + EPS)
    h_ref[...] = (hn * g_ref[...]).astype(h_ref.dtype)


def _out_proj(m, w_out, layer, x, g, tm):
    n, d = x.shape
    row = pl.BlockSpec((tm, d), lambda i: (i, 0))
    return pl.pallas_call(
        _out_proj_kernel,
        grid=(n // tm,),
        in_specs=[row, pl.BlockSpec((None, d, d), lambda i: (layer, 0, 0)), row, pl.BlockSpec((1, d), lambda i: (0, 0))],
        out_specs=[row, row],
        out_shape=[jax.ShapeDtypeStruct((n, d), F32), jax.ShapeDtypeStruct((n, d), BF16)],
        compiler_params=_params("parallel"),
        name="out_proj",
    )(m, w_out, x, g.reshape(1, d))


def _mlp_kernel(h_ref, wu_ref, wd_ref, x_ref, g_ref, y_ref, hn_ref, acc):
    f = pl.program_id(1)
    u = jnp.maximum(jnp.dot(h_ref[...], wu_ref[...], preferred_element_type=F32), 0.0)
    part = jnp.dot((u * u).astype(BF16), wd_ref[...], preferred_element_type=F32)

    @pl.when(f == 0)
    def _():
        acc[...] = part

    @pl.when(f > 0)
    def _():
        acc[...] = acc[...] + part

    @pl.when(f == pl.num_programs(1) - 1)
    def _():
        y = x_ref[...] + acc[...]
        y_ref[...] = y
        hn = y * lax.rsqrt(jnp.mean(y * y, axis=-1, keepdims=True) + EPS)
        hn_ref[...] = (hn * g_ref[...]).astype(hn_ref.dtype)


def _mlp(h, w_up, w_down, layer, x, g_next, tm, tf):
    n, d = x.shape
    dff = w_up.shape[2]
    row = pl.BlockSpec((tm, d), lambda i, f: (i, 0))
    return pl.pallas_call(
        _mlp_kernel,
        grid=(n // tm, dff // tf),
        in_specs=[row, pl.BlockSpec((None, d, tf), lambda i, f: (layer, 0, f)),
                  pl.BlockSpec((None, tf, d), lambda i, f: (layer, f, 0)), row,
                  pl.BlockSpec((1, d), lambda i, f: (0, 0))],
        out_specs=[row, row],
        out_shape=[jax.ShapeDtypeStruct((n, d), F32), jax.ShapeDtypeStruct((n, d), BF16)],
        scratch_shapes=[pltpu.VMEM((tm, d), F32)],
        compiler_params=_params("parallel", "arbitrary"),
        name="mlp",
    )(h, w_up, w_down, x, g_next.reshape(1, d))


def _tiles(n):
    return (min(n, 1024), min(n, 512), min(n, 256), min(n, 512))


def _trunk_layer(layer, x, h, shape, w, lb, next_gain, attn_fn, hgrn_fn, mem_fn):
    b, l = shape
    n, d = x.shape
    t_proj, t_merge, t_out, t_mlp = _tiles(n)
    n_heads = w["hg_width"] // HEAD_DIM
    qk_gain = jnp.concatenate([jnp.tile(w["q_norm_moba"][layer], n_heads), jnp.tile(w["k_norm_moba"][layer], n_heads)])
    p2 = _proj(h, w["w_in"], layer, qk_gain.reshape(1, -1), HEAD_DIM, t_proj, 1024)
    p3 = p2.reshape(b, l, -1)
    o_a = attn_fn(p3)
    o_h, s_new = hgrn_fn(p3, lb)
    o_m = mem_fn(p3)
    flat = lambda a: a.reshape(n, -1)
    m = _merge(flat(o_a), flat(o_h), flat(o_m), p2, w["gate_col0"], w["w_branch"], layer, t_merge, 512)
    x, h2 = _out_proj(m, w["w_out"], layer, x, w["ln_mlp"][layer], t_out)
    x, h_next = _mlp(h2, w["w_up"], w["w_down"], layer, x, next_gain, t_mlp, 1024)
    return x, h_next, p3, s_new


def kernel(x_prompt, x_sample, mem_prompt, cache_moba_k, cache_moba_v, cache_mem_k, cache_mem_v, state_hgrn,
           page_table, ln_mix, w_in, q_norm_moba, k_norm_moba, lb_logits, hg_out_norm, q_norm_mem, ln_mem,
           w_mem_kv, k_norm_mem, w_branch, w_out, ln_mlp, w_up, w_down):
    depth = w_in.shape[0]
    bp, lp, d = x_prompt.shape
    bs, ls, _ = x_sample.shape
    hg_width = lb_logits.shape[1]
    n_heads = hg_width // HEAD_DIM
    moba_width = n_heads * HEAD_DIM
    mem_width = cache_mem_k.shape[3] * cache_mem_k.shape[4]
    mem_tokens = mem_prompt.shape[1]
    hg_col0 = 3 * moba_width
    mem_col0 = hg_col0 + 4 * hg_width
    gate_col0 = mem_col0 + mem_width

    w = dict(
        w_in=w_in.astype(BF16), w_branch=w_branch.astype(BF16), w_out=w_out.astype(BF16),
        w_up=w_up.astype(BF16), w_down=w_down.astype(BF16), ln_mlp=ln_mlp,
        q_norm_moba=q_norm_moba, k_norm_moba=k_norm_moba, hg_width=hg_width, gate_col0=gate_col0,
    )
    w_mem = w_mem_kv.astype(BF16)
    lower = _lower_bounds(lb_logits)

    ck = cache_moba_k.reshape(cache_moba_k.shape[:2] + (-1, HEAD_DIM))
    cv = cache_moba_v.reshape(cache_moba_v.shape[:2] + (-1, HEAD_DIM))
    cmk = cache_mem_k.reshape(cache_mem_k.shape[:3] + (mem_width,))
    cmv = cache_mem_v.reshape(cache_mem_v.shape[:3] + (mem_width,))

    xp = x_prompt.reshape(bp * lp, d)
    xs = x_sample.reshape(bs * ls, d)
    hp = _rmsnorm_bf16(xp, ln_mix[0], min(bp * lp, 512))
    hs = _rmsnorm_bf16(xs, ln_mix[0], bs * ls)
    s0_p = jnp.zeros((bp,) + state_hgrn.shape[2:], F32)
    mem_flat = mem_prompt.reshape(bp * mem_tokens, d)

    kp_l, vp_l, mkp_l, mvp_l, sp_l, ks_l, vs_l, ss_l = [], [], [], [], [], [], [], []
    for layer in range(depth):
        next_gain = ln_mix[layer + 1] if layer + 1 < depth else ln_mix[layer]
        gn_h = hg_out_norm[layer]
        gn_m = q_norm_mem[layer]

        hm = _rmsnorm_bf16(mem_flat, ln_mem[layer], min(bp * mem_tokens, 512))
        mem_gain = jnp.tile(k_norm_mem[layer], MEM_HEADS).reshape(1, -1)
        mkv = _proj(hm, w_mem, layer, mem_gain, mem_width // MEM_HEADS, min(bp * mem_tokens, 512), 1024)
        mk = mkv[:, :mem_width].reshape(1, bp, mem_tokens, mem_width)
        mv = mkv[:, mem_width:].reshape(1, bp, mem_tokens, mem_width)

        xp, hp, pp, sp = _trunk_layer(
            layer, xp, hp, (bp, lp), w, lower[layer], next_gain,
            lambda p3: _moba_prompt(p3, n_heads),
            lambda p3, lb: _hgrn_prompt(p3, lb, gn_h, s0_p, n_heads, hg_col0),
            lambda p3: _mem_attn(p3, mk, mv, 0, gn_m, mem_col0, min(lp, 512), BF16))
        xs, hs, ps, ss = _trunk_layer(
            layer, xs, hs, (bs, ls), w, lower[layer], next_gain,
            lambda p3: _moba_sample(p3, ck, cv, layer, page_table, n_heads),
            lambda p3, lb: _hgrn_sample(p3, lb, gn_h, state_hgrn, layer, n_heads, hg_col0),
            lambda p3: _mem_attn(p3, cmk, cmv, layer, gn_m, mem_col0, ls, F32))

        kv_shape = lambda p3: p3.shape[:2] + (n_heads, HEAD_DIM)
        kp_l.append(pp[:, :, moba_width:2 * moba_width].reshape(kv_shape(pp)))
        vp_l.append(pp[:, :, 2 * moba_width:3 * moba_width].reshape(kv_shape(pp)))
        ks_l.append(ps[:, :, moba_width:2 * moba_width].reshape(kv_shape(ps)))
        vs_l.append(ps[:, :, 2 * moba_width:3 * moba_width].reshape(kv_shape(ps)))
        mem_shape = (bp, mem_tokens, MEM_HEADS, mem_width // MEM_HEADS)
        mkp_l.append(mk.reshape(mem_shape))
        mvp_l.append(mv.reshape(mem_shape))
        sp_l.append(sp)
        ss_l.append(ss)

    return (xp.reshape(bp, lp, d), xs.reshape(bs, ls, d), jnp.stack(kp_l), jnp.stack(vp_l), jnp.stack(mkp_l),
            jnp.stack(mvp_l), jnp.stack(sp_l), jnp.stack(ks_l), jnp.stack(vs_l), jnp.stack(ss_l))
```

```python
import functools

import jax
import jax.numpy as jnp
from jax import lax
from jax.experimental import pallas as pl
from jax.experimental.pallas import tpu as pltpu

F32 = jnp.float32
BF16 = jnp.bfloat16

EPS = 1e-6
MASK_FILL = -1e30
MOBA_BLOCK = 256
MOBA_TOPK = 3
HEAD_DIM = 128
MEM_HEADS = 4
MOBA_QB = 2
LOG2E = 1.4426950408889634
HG_CHUNK = 64
HG_SUB = 8
HG_STEP = 256
HG_HEADS_PER_STEP = 4
HG_MAX_GROW = 100.0
PAGES_PER_STEP = 8
VMEM_LIMIT_BYTES = 56 * 1024 * 1024

_NT = (((1,), (1,)), ((), ()))
_TN = (((0,), (0,)), ((), ()))


def _params(*sem):
    return pltpu.CompilerParams(dimension_semantics=sem, vmem_limit_bytes=VMEM_LIMIT_BYTES)


def _sigmoid(x):
    return 1.0 / (1.0 + jnp.exp(-x))


def _lower_bounds_kernel(lg_ref, out_ref):
    x = lg_ref[...]
    e = jnp.exp(x - jnp.max(x, axis=0, keepdims=True))
    p = e / jnp.sum(e, axis=0, keepdims=True)
    c = p[0:1]
    rows = [c - p[0:1]]
    for l in range(1, x.shape[0]):
        c = c + p[l:l + 1]
        rows.append(c - p[0:1])
    out_ref[...] = jnp.concatenate(rows, axis=0)


def _lower_bounds(lb_logits):
    return pl.pallas_call(
        _lower_bounds_kernel,
        out_shape=jax.ShapeDtypeStruct(lb_logits.shape, F32),
        name="lower_bounds",
    )(lb_logits)


def _rmsnorm_kernel(x_ref, g_ref, o_ref):
    x = x_ref[...]
    y = x * lax.rsqrt(jnp.mean(x * x, axis=-1, keepdims=True) + EPS)
    o_ref[...] = (y * g_ref[...]).astype(o_ref.dtype)


def _rmsnorm_bf16(x, g, tm):
    n, d = x.shape
    return pl.pallas_call(
        _rmsnorm_kernel,
        grid=(n // tm,),
        in_specs=[pl.BlockSpec((tm, d), lambda i: (i, 0)), pl.BlockSpec((1, d), lambda i: (0, 0))],
        out_specs=pl.BlockSpec((tm, d), lambda i: (i, 0)),
        out_shape=jax.ShapeDtypeStruct((n, d), BF16),
        compiler_params=_params("parallel"),
        name="rmsnorm",
    )(x, g.reshape(1, d))


def _proj_kernel(h_ref, w_ref, g_ref, o_ref, *, n_norm_tiles, group):
    acc = jnp.dot(h_ref[...], w_ref[...], preferred_element_type=F32)
    j = pl.program_id(1)

    @pl.when(j < n_norm_tiles)
    def _():
        for c in range(acc.shape[1] // group):
            a = acc[:, c * group:(c + 1) * group]
            y = a * lax.rsqrt(jnp.mean(a * a, axis=-1, keepdims=True) + EPS)
            o_ref[:, c * group:(c + 1) * group] = y * g_ref[:, c * group:(c + 1) * group]

    @pl.when(j >= n_norm_tiles)
    def _():
        o_ref[...] = acc


def _proj(h, w_all, layer, gains, group, tm, tn):
    n, k = h.shape
    m = w_all.shape[2]
    n_norm_tiles = gains.shape[1] // tn
    kern = functools.partial(_proj_kernel, n_norm_tiles=n_norm_tiles, group=group)
    return pl.pallas_call(
        kern,
        grid=(n // tm, m // tn),
        in_specs=[
            pl.BlockSpec((tm, k), lambda i, j: (i, 0)),
            pl.BlockSpec((None, k, tn), lambda i, j: (layer, 0, j)),
            pl.BlockSpec((1, tn), lambda i, j: (0, jnp.minimum(j, n_norm_tiles - 1))),
        ],
        out_specs=pl.BlockSpec((tm, tn), lambda i, j: (i, j)),
        out_shape=jax.ShapeDtypeStruct((n, m), F32),
        compiler_params=_params("parallel", "arbitrary"),
        name="proj",
    )(h, w_all, gains)


def _moba_prompt_kernel(q_ref, k_ref, v_ref, o_ref, kbf, vt, kmean, selbias, biased, s_even, s_odd, m_scr, acc_scr,
                        *, nblk, k_sel, scale):
    t = pl.program_id(2)
    n_tiles = pl.num_programs(2) - 1
    blk = MOBA_BLOCK
    tq = MOBA_QB * blk

    @pl.when(t == 0)
    def _():
        for j in range(nblk):
            kj = k_ref[j * blk:(j + 1) * blk, :]
            kbf[j * blk:(j + 1) * blk, :] = kj.astype(BF16)
            kmean[j:j + 1, :] = jnp.mean(kj, axis=0, keepdims=True)
        for jp in range(nblk // MOBA_QB):
            vt[jp] = v_ref[jp * tq:(jp + 1) * tq, :].T.astype(BF16)
        kpos = lax.broadcasted_iota(jnp.int32, (tq, tq), 0)
        qpos = lax.broadcasted_iota(jnp.int32, (tq, tq), 1)
        biased[...] = jnp.where((kpos // blk == qpos // blk) & (kpos <= qpos), 0.0, 1.0)

    acc_scr[...] = jnp.zeros(acc_scr.shape, F32)

    @pl.when(t < n_tiles)
    def _():
        sc = lax.dot_general(kmean[...], q_ref[...], _NT, precision=lax.Precision.HIGHEST,
                             preferred_element_type=F32)
        jj = lax.broadcasted_iota(jnp.int32, sc.shape, 0)
        own = MOBA_QB * t + lax.broadcasted_iota(jnp.int32, sc.shape, 1) // blk
        past = jj < own
        sc = jnp.where(past, sc, MASK_FILL)
        rank = jnp.zeros(sc.shape, jnp.int32)
        for j2 in range(nblk):
            sj = sc[j2:j2 + 1, :]
            beats = (sj > sc) | ((sj == sc) & (j2 < jj))
            rank = rank + beats.astype(jnp.int32)
        selbias[...] = jnp.where(past & (rank < k_sel), 0.0, MASK_FILL)

    def run(s_cur, s_prev, cur):
        def finish_group(jp, l):
            p = jnp.exp2(s_prev[jp] - m_scr[1 - cur])
            acc_scr[...] += jnp.dot(vt[jp], p.astype(BF16), preferred_element_type=F32)
            return l + jnp.sum(p, axis=0, keepdims=True)

        def write_out(l):
            @pl.when(t > 0)
            def _():
                o_ref[...] = (acc_scr[...] / l).T.astype(o_ref.dtype)

        @pl.when(t < n_tiles)
        def _():
            qb = (q_ref[...] * (scale * LOG2E)).astype(BF16)

            def logits(jp):
                kj = kbf[pl.ds(pl.multiple_of(jp * tq, tq), tq), :]
                return lax.dot_general(kj, qb, _NT, preferred_element_type=F32)

            def bias(jp):
                rows = [jnp.broadcast_to(selbias[pl.ds(jp * MOBA_QB + u, 1), :], (blk, tq))
                        for u in range(MOBA_QB)]
                return jnp.concatenate(rows, axis=0)

            def both(jp, carry):
                m, l = carry
                s = logits(jp) + bias(jp)
                s_cur[jp] = s
                return jnp.maximum(m, jnp.max(s, axis=0, keepdims=True)), finish_group(jp, l)

            m, l = lax.fori_loop(0, t, both, (jnp.full((1, tq), MASK_FILL, F32), jnp.zeros((1, tq), F32)))
            write_out(l)

            s = logits(t) + bias(t) * biased[...]
            s_cur[t] = s
            m_scr[cur] = jnp.maximum(m, jnp.max(s, axis=0, keepdims=True))

        @pl.when(t == n_tiles)
        def _():
            write_out(lax.fori_loop(0, t, finish_group, jnp.zeros((1, tq), F32)))

    @pl.when(t % 2 == 0)
    def _():
        run(s_even, s_odd, 0)

    @pl.when(t % 2 == 1)
    def _():
        run(s_odd, s_even, 1)


def _moba_prompt(p3, n_heads):
    b, l, _ = p3.shape
    tq = MOBA_QB * MOBA_BLOCK
    assert l % tq == 0
    nblk = l // MOBA_BLOCK
    n_tiles = l // tq
    dh = HEAD_DIM
    kern = functools.partial(_moba_prompt_kernel, nblk=nblk, k_sel=min(MOBA_TOPK, nblk - 1), scale=dh ** -0.5)
    return pl.pallas_call(
        kern,
        grid=(b, n_heads, n_tiles + 1),
        in_specs=[
            pl.BlockSpec((None, tq, dh), lambda bb, h, i: (bb, jnp.minimum(i, n_tiles - 1), h)),
            pl.BlockSpec((None, l, dh), lambda bb, h, i: (bb, 0, n_heads + h)),
            pl.BlockSpec((None, l, dh), lambda bb, h, i: (bb, 0, 2 * n_heads + h)),
        ],
        out_specs=pl.BlockSpec((None, tq, dh), lambda bb, h, i: (bb, jnp.maximum(i - 1, 0), h)),
        out_shape=jax.ShapeDtypeStruct((b, l, n_heads * dh), BF16),
        scratch_shapes=[
            pltpu.VMEM((l, dh), BF16),
            pltpu.VMEM((n_tiles, dh, tq), BF16),
            pltpu.VMEM((nblk, dh), F32),
            pltpu.VMEM((nblk, tq), F32),
            pltpu.VMEM((tq, tq), F32),
            pltpu.VMEM((n_tiles, tq, tq), F32),
            pltpu.VMEM((n_tiles, tq, tq), F32),
            pltpu.VMEM((2, 1, tq), F32),
            pltpu.VMEM((dh, tq), F32),
        ],
        compiler_params=_params("parallel", "parallel", "arbitrary"),
        name="moba_prompt",
    )(p3, p3, p3)


def _moba_sample_kernel(pt_ref, q_ref, kn_ref, vn_ref, *rest, n_steps, n_heads, n_new, page, k_sel, scale):
    g = PAGES_PER_STEP
    k_refs = rest[:g]
    v_refs = rest[g:2 * g]
    o_ref = rest[2 * g]
    s_all, ksum, selb, headb, newpad, acc, lacc, m_scr, lnew = rest[2 * g + 1:]
    ph = pl.program_id(1)
    st = pl.program_id(2)
    dh = HEAD_DIM
    rows = n_new * n_heads
    lanes = page * n_heads
    ppb = MOBA_BLOCK // page
    n_pages = n_steps * g
    n_past = n_pages // ppb
    vw = selb.shape[2]

    def same_head(shape):
        r = lax.broadcasted_iota(jnp.int32, shape, 0)
        c = lax.broadcasted_iota(jnp.int32, shape, 1)
        return (r // n_new) == (c % n_heads)

    def qscaled():
        return (q_ref[...] * (scale * LOG2E)).astype(BF16)

    def page_bias(pidx):
        return jnp.concatenate([selb[pidx // ppb]] * (lanes // vw), axis=1)

    @pl.when(ph == 0)
    def _():
        qb16 = qscaled()
        for u in range(g):
            kp = k_refs[u][...]
            s_all[st * g + u] = lax.dot_general(qb16, kp.astype(BF16), _NT, preferred_element_type=F32)
            srow = jnp.sum(kp.reshape(page, n_heads, dh), axis=0)
            dst = pl.ds(pl.multiple_of((st * (g // ppb) + u // ppb) * n_heads, n_heads), n_heads)
            if u % ppb == 0:
                ksum[dst, :] = srow
            else:
                ksum[dst, :] = ksum[dst, :] + srow

    @pl.when((ph == 1) & (st == 0))
    def _():
        kmean = ksum[...] * (1.0 / MOBA_BLOCK)
        sc2 = lax.dot_general(q_ref[...], kmean, _NT, precision=lax.Precision.HIGHEST,
                              preferred_element_type=F32)
        sc2 = jnp.where(same_head(sc2.shape), sc2, 0.0)
        gr = lax.broadcasted_iota(jnp.int32, (n_past * n_heads, n_past), 0)
        gc = lax.broadcasted_iota(jnp.int32, (n_past * n_heads, n_past), 1)
        pick = jnp.where(gr // n_heads == gc, 1.0, 0.0)
        sc = jnp.dot(sc2, pick, precision=lax.Precision.HIGHEST, preferred_element_type=F32)
        jj = lax.broadcasted_iota(jnp.int32, sc.shape, 1)
        rank = jnp.zeros(sc.shape, jnp.int32)
        for j2 in range(n_past):
            sj = sc[:, j2:j2 + 1]
            beats = (sj > sc) | ((sj == sc) & (j2 < jj))
            rank = rank + beats.astype(jnp.int32)
        sel = jnp.where(rank < k_sel, 0.0, MASK_FILL)
        for j2 in range(n_past):
            selb[j2] = jnp.broadcast_to(sel[:, j2:j2 + 1], (rows, vw))
        headb[...] = jnp.where(same_head((rows, lanes)), 0.0, MASK_FILL)

        qb16 = qscaled()
        newpad[...] = jnp.zeros(newpad.shape, F32)
        newpad[0:rows, :] = kn_ref[...]
        s_new = lax.dot_general(qb16, newpad[...].astype(BF16), _NT, preferred_element_type=F32)
        r = lax.broadcasted_iota(jnp.int32, s_new.shape, 0)
        c = lax.broadcasted_iota(jnp.int32, s_new.shape, 1)
        s_new = jnp.where(same_head(s_new.shape) & (c // n_heads <= r % n_new), s_new, MASK_FILL)

        def running_max(pidx, mx):
            return jnp.maximum(mx, s_all[pidx] + page_bias(pidx))

        mx = lax.fori_loop(0, n_pages, running_max, jnp.full((rows, lanes), 2 * MASK_FILL, F32))
        m = jnp.maximum(jnp.max(mx + headb[...], axis=-1, keepdims=True), jnp.max(s_new, axis=-1, keepdims=True))
        m_scr[...] = m
        p_new = jnp.exp2(s_new - m)
        lnew[...] = jnp.sum(p_new, axis=-1, keepdims=True)
        newpad[...] = jnp.zeros(newpad.shape, F32)
        newpad[0:rows, :] = vn_ref[...]
        acc[...] = jnp.dot(p_new.astype(BF16), newpad[...].astype(BF16), preferred_element_type=F32)
        lacc[...] = jnp.zeros(lacc.shape, F32)

    @pl.when(ph == 1)
    def _():
        a = acc[...]
        ls = lacc[...]
        off = headb[...] - m_scr[...]
        for u in range(g):
            pidx = st * g + u
            p = jnp.exp2(s_all[pidx] + page_bias(pidx) + off)
            ls = ls + p
            a = a + jnp.dot(p.astype(BF16), v_refs[u][...].astype(BF16), preferred_element_type=F32)
        acc[...] = a
        lacc[...] = ls

    @pl.when((ph == 1) & (st == n_steps - 1))
    def _():
        o_ref[...] = acc[...] / (jnp.sum(lacc[...], axis=-1, keepdims=True) + lnew[...])


def _moba_sample(p3, cache_k, cache_v, layer, page_table, n_heads):
    b, n_new, _ = p3.shape
    dh = HEAD_DIM
    width = n_heads * dh
    lanes = cache_k.shape[2]
    page = lanes // n_heads
    n_pages = page_table.shape[1]
    g = PAGES_PER_STEP
    ppb = MOBA_BLOCK // page
    rows = n_new * n_heads
    vw = 128
    assert MOBA_BLOCK % page == 0 and g % ppb == 0 and n_pages % g == 0 and rows <= vw and lanes % vw == 0
    n_steps = n_pages // g
    n_past = n_pages // ppb
    kern = functools.partial(_moba_sample_kernel, n_steps=n_steps, n_heads=n_heads, n_new=n_new, page=page,
                             k_sel=min(MOBA_TOPK, n_past), scale=dh ** -0.5)

    heads = lambda c0: p3[:, :, c0:c0 + width].reshape(b, n_new, n_heads, dh)
    q_rows = heads(0).transpose(0, 2, 1, 3).reshape(b, rows, dh)
    kn_rows = heads(width).reshape(b, rows, dh)
    vn_rows = heads(2 * width).reshape(b, rows, dh)

    def k_map(u):
        return lambda bb, ph, st, pt: (layer, pt[bb, jnp.where(ph == 0, st, n_steps - 1) * g + u], 0, 0)

    def v_map(u):
        return lambda bb, ph, st, pt: (layer, pt[bb, jnp.where(ph == 0, 0, st) * g + u], 0, 0)

    new_spec = pl.BlockSpec((None, rows, dh), lambda bb, ph, st, pt: (bb, 0, 0))
    page_spec = lambda imap: pl.BlockSpec((None, None, lanes, dh), imap)
    grid_spec = pltpu.PrefetchScalarGridSpec(
        num_scalar_prefetch=1,
        grid=(b, 2, n_steps),
        in_specs=[new_spec, new_spec, new_spec]
                 + [page_spec(k_map(u)) for u in range(g)] + [page_spec(v_map(u)) for u in range(g)],
        out_specs=new_spec,
        scratch_shapes=[
            pltpu.VMEM((n_pages, rows, lanes), F32),
            pltpu.VMEM((n_past * n_heads, dh), F32),
            pltpu.VMEM((n_past, rows, vw), F32),
            pltpu.VMEM((rows, lanes), F32),
            pltpu.VMEM((vw, dh), F32),
            pltpu.VMEM((rows, dh), F32),
            pltpu.VMEM((rows, lanes), F32),
            pltpu.VMEM((rows, 1), F32),
            pltpu.VMEM((rows, 1), F32),
        ],
    )
    o = pl.pallas_call(
        kern,
        grid_spec=grid_spec,
        out_shape=jax.ShapeDtypeStruct((b, rows, dh), F32),
        compiler_params=_params("parallel", "arbitrary", "arbitrary"),
        name="moba_sample",
    )(page_table, q_rows, kn_rows, vn_rows, *([cache_k] * g), *([cache_v] * g))
    return o.reshape(b, n_heads, n_new, dh).transpose(0, 2, 1, 3).reshape(b, n_new, width)


def _hgrn_levels():
    sizes, h = [], HG_SUB
    while h < HG_CHUNK:
        sizes.append(h)
        h *= 2
    return sizes


def _hgrn_prompt_kernel(xq_ref, xf_ref, xi_ref, xg_ref, lb_ref, gn_ref, s0_ref, o_ref, s_ref,
                        st_scr, st_old, tril_scr, group_scr, diag_scr):
    t = pl.program_id(2)
    n = xq_ref.shape[0]
    dh = HEAD_DIM
    c = HG_CHUNK
    levels = _hgrn_levels()

    @pl.when(t == 0)
    def _():
        for hh in range(HG_HEADS_PER_STEP):
            st_scr[hh] = s0_ref[hh].T
        r = lax.broadcasted_iota(jnp.int32, (n, n), 0)
        cc = lax.broadcasted_iota(jnp.int32, (n, n), 1)
        tril_scr[...] = jnp.where((cc <= r) & (r // c == cc // c), 1.0, 0.0).astype(BF16)
        diag_scr[...] = jnp.where((cc <= r) & (r // HG_SUB == cc // HG_SUB), 1.0, 0.0)
        for li, hs in enumerate(levels):
            group_scr[li] = jnp.where(r // (2 * hs) == cc // (2 * hs), 1.0, 0.0)

    heads = range(HG_HEADS_PER_STEP)
    cols = [slice(hh * dh, (hh + 1) * dh) for hh in heads]

    def finish(hh, o):
        y = o * lax.rsqrt(jnp.mean(o * o, axis=-1, keepdims=True) + EPS) * gn_ref[...]
        o_ref[:, cols[hh]] = (y * _sigmoid(xg_ref[:, cols[hh]])).astype(o_ref.dtype)

    qkg = [_hgrn_gates(xq_ref[:, cs], xf_ref[:, cs], lb_ref[:, cs]) for cs in cols]
    vb = [xi_ref[:, cs].astype(BF16) for cs in cols]
    b = [_hgrn_cumsum(g2, tril_scr) for _, _, g2 in qkg]
    grow = [_hgrn_tile_grow(bh) for bh in b]
    a = [_hgrn_level_weights(q, k, bh, group_scr) for (q, k, _), bh in zip(qkg, b)]
    ad = [_hgrn_tile_weights(q, k, jnp.minimum(gh, HG_MAX_GROW), diag_scr) for (q, k, _), gh in zip(qkg, grow)]
    o = [jnp.dot((ah + adh).astype(BF16), vh, preferred_element_type=F32) for ah, adh, vh in zip(a, ad, vb)]
    for hh in heads:
        st_old[hh] = st_scr[hh]
        oh, st = _hgrn_carry(qkg[hh][0], qkg[hh][1], b[hh], vb[hh], o[hh], st_scr[hh])
        st_scr[hh] = st
        finish(hh, oh)

    for hh in heads:
        @pl.when(jnp.max(grow[hh]) > HG_MAX_GROW)
        def _(hh=hh):
            cs = cols[hh]
            q, k, g2 = _hgrn_gates(xq_ref[:, cs], xf_ref[:, cs], lb_ref[:, cs])
            v = xi_ref[:, cs]
            bh = _hgrn_cumsum(g2, tril_scr)
            o_lv = jnp.dot(_hgrn_level_weights(q, k, bh, group_scr).astype(BF16), v.astype(BF16),
                           preferred_element_type=F32)
            oh, _ = _hgrn_carry(q, k, bh, v.astype(BF16), o_lv + _hgrn_tile_pairs(q, k, bh, v), st_old[hh])
            finish(hh, oh)

    @pl.when(t == pl.num_programs(2) - 1)
    def _():
        for hh in heads:
            s_ref[hh] = st_scr[hh].T


def _hgrn_gates(xq, xf, lb):
    f = lb + (1.0 - lb) * _sigmoid(xf)
    return xq * _sigmoid(xq), 1.0 - f, jnp.log2(f)


def _hgrn_cumsum(g2, tril_scr):
    b = None
    resid = g2
    for _ in range(3):
        part = resid.astype(BF16)
        resid = resid - part.astype(F32)
        term = jnp.dot(tril_scr[...], part, preferred_element_type=F32)
        b = term if b is None else b + term
    return b


def _hgrn_level_weights(q, k, b, group_scr):
    n, dh = q.shape
    a = None
    for li, hs in enumerate(_hgrn_levels()):
        zero = jnp.zeros((hs, dh), F32)
        q_rows, k_rows = [], []
        for g0 in range(0, n, 2 * hs):
            first, second = slice(g0, g0 + hs), slice(g0 + hs, g0 + 2 * hs)
            bref = b[g0 + hs - 1:g0 + hs, :]
            q_rows += [zero, q[second] * jnp.exp2(b[second] - bref)]
            k_rows += [k[first] * jnp.exp2(bref - b[first]), zero]
        ql = jnp.concatenate(q_rows, axis=0).astype(BF16)
        kl = jnp.concatenate(k_rows, axis=0).astype(BF16)
        al = lax.dot_general(ql, kl, _NT, preferred_element_type=F32) * group_scr[li]
        a = al if a is None else a + al
    return a


def _hgrn_tile_grow(b):
    n, dh = b.shape
    zero_row = jnp.zeros((1, dh), F32)
    b0 = jnp.concatenate([jnp.broadcast_to(zero_row if r0 % HG_CHUNK == 0 else b[r0 - 1:r0, :], (HG_SUB, dh))
                          for r0 in range(0, n, HG_SUB)], axis=0)
    return b0 - b


def _hgrn_tile_weights(q, k, grow, diag_scr):
    qd = (q * jnp.exp2(-grow)).astype(BF16)
    kd = (k * jnp.exp2(grow)).astype(BF16)
    return lax.dot_general(qd, kd, _NT, preferred_element_type=F32) * diag_scr[...]


def _hgrn_tile_pairs(q, k, b, v):
    n, dh = q.shape
    sub = HG_SUB
    q3, k3, b3, v3 = (x.reshape(n // sub, sub, dh) for x in (q, k, b, v))
    tio = lax.broadcasted_iota(jnp.int32, q3.shape, 1)
    od = jnp.zeros(q3.shape, F32)
    for s in range(sub):
        e = jnp.exp2(jnp.where(tio >= s, b3 - b3[:, s:s + 1, :], MASK_FILL))
        w = q3 * k3[:, s:s + 1, :] * e
        od = od + jnp.sum(w, axis=-1, keepdims=True) * v3[:, s:s + 1, :]
    return od.reshape(n, dh)


def _hgrn_carry(q, k, b, vb, o, st):
    n = q.shape[0]
    c = HG_CHUNK
    parts = []
    for c0 in range(0, n, c):
        rows = slice(c0, c0 + c)
        bc = b[rows]
        b_end = bc[c - 1:c, :]
        qd = (q[rows] * jnp.exp2(bc)).astype(BF16)
        parts.append(o[rows] + lax.dot_general(qd, st.astype(BF16), _NT, preferred_element_type=F32))
        kd = (k[rows] * jnp.exp2(b_end - bc)).astype(BF16)
        st = st * jnp.exp2(b_end) + lax.dot_general(vb[rows], kd, _TN, preferred_element_type=F32)
    return jnp.concatenate(parts, axis=0), st


def _hgrn_prompt(p3, lb, gn, s0, n_heads, col0):
    b, l, _ = p3.shape
    dh = HEAD_DIM
    step = min(HG_STEP, l)
    assert l % step == 0 and step % HG_CHUNK == 0
    hps = HG_HEADS_PER_STEP
    assert n_heads % hps == 0 and col0 % (hps * dh) == 0
    c0 = col0 // (hps * dh)
    spec = lambda off: pl.BlockSpec((None, step, hps * dh), lambda bb, h, t: (bb, t, c0 + off * (n_heads // hps) + h))
    vec = pl.BlockSpec((1, hps * dh), lambda bb, h, t: (0, h))
    state = pl.BlockSpec((None, hps, dh, dh), lambda bb, h, t: (bb, h, 0, 0))
    return pl.pallas_call(
        _hgrn_prompt_kernel,
        grid=(b, n_heads // hps, l // step),
        in_specs=[spec(0), spec(1), spec(2), spec(3), vec, pl.BlockSpec((1, dh), lambda bb, h, t: (0, 0)), state],
        out_specs=[pl.BlockSpec((None, step, hps * dh), lambda bb, h, t: (bb, t, h)), state],
        out_shape=[jax.ShapeDtypeStruct((b, l, n_heads * dh), BF16), jax.ShapeDtypeStruct(s0.shape, F32)],
        scratch_shapes=[pltpu.VMEM((hps, dh, dh), F32),
                        pltpu.VMEM((hps, dh, dh), F32),
                        pltpu.VMEM((step, step), BF16),
                        pltpu.VMEM((len(_hgrn_levels()), step, step), F32),
                        pltpu.VMEM((step, step), F32)],
        compiler_params=_params("parallel", "parallel", "arbitrary"),
        name="hgrn_prompt",
    )(p3, p3, p3, p3, lb.reshape(1, -1), gn.reshape(1, dh), s0)


def _hgrn_sample_kernel(xq_ref, xf_ref, xi_ref, xg_ref, lb_ref, gn_ref, s0_ref, o_ref, s_ref):
    n_new, dh = xq_ref.shape
    xq = xq_ref[...]
    f = lb_ref[...] + (1.0 - lb_ref[...]) * _sigmoid(xf_ref[...])
    q = xq * _sigmoid(xq)
    pad = jnp.zeros((8 - n_new % 8, dh), F32)
    cols = lambda a: jnp.concatenate([a, pad], axis=0).T
    f_c, k_c, q_c = cols(f), cols(1.0 - f), cols(q)
    v = xi_ref[...]
    s = s0_ref[...]
    outs = []
    for t in range(n_new):
        s = f_c[:, t:t + 1] * s + k_c[:, t:t + 1] * v[t:t + 1, :]
        outs.append(jnp.sum(q_c[:, t:t + 1] * s, axis=0, keepdims=True))
    o = jnp.concatenate(outs, axis=0)
    y = o * lax.rsqrt(jnp.mean(o * o, axis=-1, keepdims=True) + EPS) * gn_ref[...]
    o_ref[...] = y * _sigmoid(xg_ref[...])
    s_ref[...] = s


def _hgrn_sample(p3, lb, gn, state_all, layer, n_heads, col0):
    b, n_new, _ = p3.shape
    dh = HEAD_DIM
    c0 = col0 // dh
    spec = lambda off: pl.BlockSpec((None, n_new, dh), lambda bb, h: (bb, 0, c0 + off * n_heads + h))
    return pl.pallas_call(
        _hgrn_sample_kernel,
        grid=(b, n_heads),
        in_specs=[spec(0), spec(1), spec(2), spec(3),
                  pl.BlockSpec((1, dh), lambda bb, h: (0, h)), pl.BlockSpec((1, dh), lambda bb, h: (0, 0)),
                  pl.BlockSpec((None, None, None, dh, dh), lambda bb, h: (layer, bb, h, 0, 0))],
        out_specs=[pl.BlockSpec((None, n_new, dh), lambda bb, h: (bb, 0, h)),
                   pl.BlockSpec((None, None, dh, dh), lambda bb, h: (bb, h, 0, 0))],
        out_shape=[jax.ShapeDtypeStruct((b, n_new, n_heads * dh), F32),
                   jax.ShapeDtypeStruct(state_all.shape[1:], F32)],
        compiler_params=_params("parallel", "parallel"),
        name="hgrn_sample",
    )(p3, p3, p3, p3, lb.reshape(1, -1), gn.reshape(1, dh), state_all)


def _mem_attn_kernel(q_ref, k_ref, v_ref, g_ref, o_ref, *, scale):
    dm = g_ref.shape[1]
    for h in range(q_ref.shape[1] // dm):
        cols = slice(h * dm, (h + 1) * dm)
        q = q_ref[:, cols]
        qn = q * lax.rsqrt(jnp.mean(q * q, axis=-1, keepdims=True) + EPS) * g_ref[...]
        s = lax.dot_general(qn.astype(BF16), k_ref[:, cols].astype(BF16), _NT, preferred_element_type=F32) * scale
        p = jnp.exp(s - jnp.max(s, axis=-1, keepdims=True))
        o = jnp.dot(p.astype(BF16), v_ref[:, cols].astype(BF16), preferred_element_type=F32)
        o_ref[:, cols] = (o / jnp.sum(p, axis=-1, keepdims=True)).astype(o_ref.dtype)


def _mem_attn(p3, mem_k, mem_v, layer, gn, col0, tq, out_dtype):
    b, l, _ = p3.shape
    mtok, width = mem_k.shape[2], mem_k.shape[3]
    dm = width // MEM_HEADS
    kern = functools.partial(_mem_attn_kernel, scale=dm ** -0.5)
    kv = pl.BlockSpec((None, None, mtok, width), lambda bb, i: (layer, bb, 0, 0))
    return pl.pallas_call(
        kern,
        grid=(b, l // tq),
        in_specs=[pl.BlockSpec((None, tq, width), lambda bb, i: (bb, i, col0 // width)), kv, kv,
                  pl.BlockSpec((1, dm), lambda bb, i: (0, 0))],
        out_specs=pl.BlockSpec((None, tq, width), lambda bb, i: (bb, i, 0)),
        out_shape=jax.ShapeDtypeStruct((b, l, width), out_dtype),
        compiler_params=_params("parallel", "arbitrary"),
        name="mem_attn",
    )(p3, mem_k, mem_v, gn.reshape(1, dm))


def _merge_kernel(oa_ref, oh_ref, om_ref, ga_ref, gh_ref, gm_ref, wa_ref, wh_ref, wm_ref, o_ref):
    acc = _sigmoid(ga_ref[...]) * jnp.dot(oa_ref[...].astype(BF16), wa_ref[...], preferred_element_type=F32)
    acc = acc + _sigmoid(gh_ref[...]) * jnp.dot(oh_ref[...].astype(BF16), wh_ref[...], preferred_element_type=F32)
    acc = acc + _sigmoid(gm_ref[...]) * jnp.dot(om_ref[...].astype(BF16), wm_ref[...], preferred_element_type=F32)
    o_ref[...] = acc.astype(o_ref.dtype)


def _merge(o_a, o_h, o_m, p2, gate_col0, w_branch, layer, tm, tn):
    n, bw = o_a.shape
    d = w_branch.shape[3]
    gspec = lambda br: pl.BlockSpec((tm, tn), lambda i, j: (i, (gate_col0 + br * d) // tn + j))
    wspec = lambda br: pl.BlockSpec((None, None, bw, tn), lambda i, j: (layer, br, 0, j))
    ospec = pl.BlockSpec((tm, bw), lambda i, j: (i, 0))
    return pl.pallas_call(
        _merge_kernel,
        grid=(n // tm, d // tn),
        in_specs=[ospec, ospec, ospec, gspec(0), gspec(1), gspec(2), wspec(0), wspec(1), wspec(2)],
        out_specs=pl.BlockSpec((tm, tn), lambda i, j: (i, j)),
        out_shape=jax.ShapeDtypeStruct((n, d), BF16),
        compiler_params=_params("parallel", "arbitrary"),
        name="merge",
    )(o_a, o_h, o_m, p2, p2, p2, w_branch, w_branch, w_branch)


def _out_proj_kernel(m_ref, w_ref, x_ref, g_ref, y_ref, h_ref):
    y = x_ref[...] + jnp.dot(m_ref[...], w_ref[...], preferred_element_type=F32)
    y_ref[...] = y
    hn = y * lax.rsqrt(jnp.mean(y * y, axis=-1, keepdims=True) + EPS)
    h_ref[...] = (hn * g_ref[...]).astype(h_ref.dtype)


def _out_proj(m, w_out, layer, x, g, tm):
    n, d = x.shape
    row = pl.BlockSpec((tm, d), lambda i: (i, 0))
    return pl.pallas_call(
        _out_proj_kernel,
        grid=(n // tm,),
        in_specs=[row, pl.BlockSpec((None, d, d), lambda i: (layer, 0, 0)), row, pl.BlockSpec((1, d), lambda i: (0, 0))],
        out_specs=[row, row],
        out_shape=[jax.ShapeDtypeStruct((n, d), F32), jax.ShapeDtypeStruct((n, d), BF16)],
        compiler_params=_params("parallel"),
        name="out_proj",
    )(m, w_out, x, g.reshape(1, d))


def _mlp_kernel(h_ref, wu_ref, wd_ref, x_ref, g_ref, y_ref, hn_ref, acc):
    f = pl.program_id(1)
    u = jnp.maximum(jnp.dot(h_ref[...], wu_ref[...], preferred_element_type=F32), 0.0)
    part = jnp.dot((u * u).astype(BF16), wd_ref[...], preferred_element_type=F32)

    @pl.when(f == 0)
    def _():
        acc[...] = part

    @pl.when(f > 0)
    def _():
        acc[...] = acc[...] + part

    @pl.when(f == pl.num_programs(1) - 1)
    def _():
        y = x_ref[...] + acc[...]
        y_ref[...] = y
        hn = y * lax.rsqrt(jnp.mean(y * y, axis=-1, keepdims=True) + EPS)
        hn_ref[...] = (hn * g_ref[...]).astype(hn_ref.dtype)


def _mlp(h, w_up, w_down, layer, x, g_next, tm, tf):
    n, d = x.shape
    dff = w_up.shape[2]
    row = pl.BlockSpec((tm, d), lambda i, f: (i, 0))
    return pl.pallas_call(
        _mlp_kernel,
        grid=(n // tm, dff // tf),
        in_specs=[row, pl.BlockSpec((None, d, tf), lambda i, f: (layer, 0, f)),
                  pl.BlockSpec((None, tf, d), lambda i, f: (layer, f, 0)), row,
                  pl.BlockSpec((1, d), lambda i, f: (0, 0))],
        out_specs=[row, row],
        out_shape=[jax.ShapeDtypeStruct((n, d), F32), jax.ShapeDtypeStruct((n, d), BF16)],
        scratch_shapes=[pltpu.VMEM((tm, d), F32)],
        compiler_params=_params("parallel", "arbitrary"),
        name="mlp",
    )(h, w_up, w_down, x, g_next.reshape(1, d))


def _tiles(n):
    return (min(n, 1024), min(n, 512), min(n, 256), min(n, 512))


def _trunk_layer(layer, x, h, shape, w, lb, next_gain, attn_fn, hgrn_fn, mem_fn):
    b, l = shape
    n, d = x.shape
    t_proj, t_merge, t_out, t_mlp = _tiles(n)
    n_heads = w["hg_width"] // HEAD_DIM
    qk_gain = jnp.concatenate([jnp.tile(w["q_norm_moba"][layer], n_heads), jnp.tile(w["k_norm_moba"][layer], n_heads)])
    p2 = _proj(h, w["w_in"], layer, qk_gain.reshape(1, -1), HEAD_DIM, t_proj, 1024)
    p3 = p2.reshape(b, l, -1)
    o_a = attn_fn(p3)
    o_h, s_new = hgrn_fn(p3, lb)
    o_m = mem_fn(p3)
    flat = lambda a: a.reshape(n, -1)
    m = _merge(flat(o_a), flat(o_h), flat(o_m), p2, w["gate_col0"], w["w_branch"], layer, t_merge, 512)
    x, h2 = _out_proj(m, w["w_out"], layer, x, w["ln_mlp"][layer], t_out)
    x, h_next = _mlp(h2, w["w_up"], w["w_down"], layer, x, next_gain, t_mlp, 1024)
    return x, h_next, p3, s_new


def kernel(x_prompt, x_sample, mem_prompt, cache_moba_k, cache_moba_v, cache_mem_k, cache_mem_v, state_hgrn,
           page_table, ln_mix, w_in, q_norm_moba, k_norm_moba, lb_logits, hg_out_norm, q_norm_mem, ln_mem,
           w_mem_kv, k_norm_mem, w_branch, w_out, ln_mlp, w_up, w_down):
    depth = w_in.shape[0]
    bp, lp, d = x_prompt.shape
    bs, ls, _ = x_sample.shape
    hg_width = lb_logits.shape[1]
    n_heads = hg_width // HEAD_DIM
    moba_width = n_heads * HEAD_DIM
    mem_width = cache_mem_k.shape[3] * cache_mem_k.shape[4]
    mem_tokens = mem_prompt.shape[1]
    hg_col0 = 3 * moba_width
    mem_col0 = hg_col0 + 4 * hg_width
    gate_col0 = mem_col0 + mem_width

    w = dict(
        w_in=w_in.astype(BF16), w_branch=w_branch.astype(BF16), w_out=w_out.astype(BF16),
        w_up=w_up.astype(BF16), w_down=w_down.astype(BF16), ln_mlp=ln_mlp,
        q_norm_moba=q_norm_moba, k_norm_moba=k_norm_moba, hg_width=hg_width, gate_col0=gate_col0,
    )
    w_mem = w_mem_kv.astype(BF16)
    lower = _lower_bounds(lb_logits)

    ck = cache_moba_k.reshape(cache_moba_k.shape[:2] + (-1, HEAD_DIM))
    cv = cache_moba_v.reshape(cache_moba_v.shape[:2] + (-1, HEAD_DIM))
    cmk = cache_mem_k.reshape(cache_mem_k.shape[:3] + (mem_width,))
    cmv = cache_mem_v.reshape(cache_mem_v.shape[:3] + (mem_width,))

    xp = x_prompt.reshape(bp * lp, d)
    xs = x_sample.reshape(bs * ls, d)
    hp = _rmsnorm_bf16(xp, ln_mix[0], min(bp * lp, 512))
    hs = _rmsnorm_bf16(xs, ln_mix[0], bs * ls)
    s0_p = jnp.zeros((bp,) + state_hgrn.shape[2:], F32)
    mem_flat = mem_prompt.reshape(bp * mem_tokens, d)

    kp_l, vp_l, mkp_l, mvp_l, sp_l, ks_l, vs_l, ss_l = [], [], [], [], [], [], [], []
    for layer in range(depth):
        next_gain = ln_mix[layer + 1] if layer + 1 < depth else ln_mix[layer]
        gn_h = hg_out_norm[layer]
        gn_m = q_norm_mem[layer]

        hm = _rmsnorm_bf16(mem_flat, ln_mem[layer], min(bp * mem_tokens, 512))
        mem_gain = jnp.tile(k_norm_mem[layer], MEM_HEADS).reshape(1, -1)
        mkv = _proj(hm, w_mem, layer, mem_gain, mem_width // MEM_HEADS, min(bp * mem_tokens, 512), 1024)
        mk = mkv[:, :mem_width].reshape(1, bp, mem_tokens, mem_width)
        mv = mkv[:, mem_width:].reshape(1, bp, mem_tokens, mem_width)

        xp, hp, pp, sp = _trunk_layer(
            layer, xp, hp, (bp, lp), w, lower[layer], next_gain,
            lambda p3: _moba_prompt(p3, n_heads),
            lambda p3, lb: _hgrn_prompt(p3, lb, gn_h, s0_p, n_heads, hg_col0),
            lambda p3: _mem_attn(p3, mk, mv, 0, gn_m, mem_col0, min(lp, 512), BF16))
        xs, hs, ps, ss = _trunk_layer(
            layer, xs, hs, (bs, ls), w, lower[layer], next_gain,
            lambda p3: _moba_sample(p3, ck, cv, layer, page_table, n_heads),
            lambda p3, lb: _hgrn_sample(p3, lb, gn_h, state_hgrn, layer, n_heads, hg_col0),
            lambda p3: _mem_attn(p3, cmk, cmv, layer, gn_m, mem_col0, ls, F32))

        kv_shape = lambda p3: p3.shape[:2] + (n_heads, HEAD_DIM)
        kp_l.append(pp[:, :, moba_width:2 * moba_width].reshape(kv_shape(pp)))
        vp_l.append(pp[:, :, 2 * moba_width:3 * moba_width].reshape(kv_shape(pp)))
        ks_l.append(ps[:, :, moba_width:2 * moba_width].reshape(kv_shape(ps)))
        vs_l.append(ps[:, :, 2 * moba_width:3 * moba_width].reshape(kv_shape(ps)))
        mem_shape = (bp, mem_tokens, MEM_HEADS, mem_width // MEM_HEADS)
        mkp_l.append(mk.reshape(mem_shape))
        mvp_l.append(mv.reshape(mem_shape))
        sp_l.append(sp)
        ss_l.append(ss)

    return (xp.reshape(bp, lp, d), xs.reshape(bs, ls, d), jnp.stack(kp_l), jnp.stack(vp_l), jnp.stack(mkp_l),
            jnp.stack(mvp_l), jnp.stack(sp_l), jnp.stack(ks_l), jnp.stack(vs_l), jnp.stack(ss_l))
```

```python
import functools

import jax
import jax.numpy as jnp
from jax import lax
from jax.experimental import pallas as pl
from jax.experimental.pallas import tpu as pltpu

F32 = jnp.float32
BF16 = jnp.bfloat16

EPS = 1e-6
MASK_FILL = -1e30
MOBA_BLOCK = 256
MOBA_TOPK = 3
HEAD_DIM = 128
MEM_HEADS = 4
MOBA_QB = 2
LOG2E = 1.4426950408889634
HG_CHUNK = 64
HG_SUB = 8
HG_STEP = 256
HG_HEADS_PER_STEP = 4
HG_MAX_GROW = 100.0
PAGES_PER_STEP = 8
VMEM_LIMIT_BYTES = 56 * 1024 * 1024

_NT = (((1,), (1,)), ((), ()))
_TN = (((0,), (0,)), ((), ()))


def _params(*sem):
    return pltpu.CompilerParams(dimension_semantics=sem, vmem_limit_bytes=VMEM_LIMIT_BYTES)


def _sigmoid(x):
    return 1.0 / (1.0 + jnp.exp(-x))


def _lower_bounds_kernel(lg_ref, out_ref):
    x = lg_ref[...]
    e = jnp.exp(x - jnp.max(x, axis=0, keepdims=True))
    p = e / jnp.sum(e, axis=0, keepdims=True)
    c = p[0:1]
    rows = [c - p[0:1]]
    for l in range(1, x.shape[0]):
        c = c + p[l:l + 1]
        rows.append(c - p[0:1])
    out_ref[...] = jnp.concatenate(rows, axis=0)


def _lower_bounds(lb_logits):
    return pl.pallas_call(
        _lower_bounds_kernel,
        out_shape=jax.ShapeDtypeStruct(lb_logits.shape, F32),
        name="lower_bounds",
    )(lb_logits)


def _rmsnorm_kernel(x_ref, g_ref, o_ref):
    x = x_ref[...]
    y = x * lax.rsqrt(jnp.mean(x * x, axis=-1, keepdims=True) + EPS)
    o_ref[...] = (y * g_ref[...]).astype(o_ref.dtype)


def _rmsnorm_bf16(x, g, tm):
    n, d = x.shape
    return pl.pallas_call(
        _rmsnorm_kernel,
        grid=(n // tm,),
        in_specs=[pl.BlockSpec((tm, d), lambda i: (i, 0)), pl.BlockSpec((1, d), lambda i: (0, 0))],
        out_specs=pl.BlockSpec((tm, d), lambda i: (i, 0)),
        out_shape=jax.ShapeDtypeStruct((n, d), BF16),
        compiler_params=_params("parallel"),
        name="rmsnorm",
    )(x, g.reshape(1, d))


def _proj_kernel(h_ref, w_ref, g_ref, o_ref, *copies, n_norm_tiles, group):
    o_ref[...] = jnp.dot(h_ref[...], w_ref[...].astype(BF16), preferred_element_type=F32)
    j = pl.program_id(1)

    @pl.when(j < n_norm_tiles)
    def _():
        for c in range(o_ref.shape[1] // group):
            a = o_ref[:, c * group:(c + 1) * group]
            y = a * lax.rsqrt(jnp.mean(a * a, axis=-1, keepdims=True) + EPS)
            o_ref[:, c * group:(c + 1) * group] = y * g_ref[:, c * group:(c + 1) * group]

    for c, copy_ref in enumerate(copies):
        @pl.when(j == c + 1)
        def _(copy_ref=copy_ref):
            copy_ref[...] = o_ref[...]


def _proj(h, w_all, layer, gains, group, tm, tn, n_copies=0):
    n, k = h.shape
    m = w_all.shape[2]
    n_norm_tiles = gains.shape[1] // tn
    kern = functools.partial(_proj_kernel, n_norm_tiles=n_norm_tiles, group=group)
    outs = pl.pallas_call(
        kern,
        grid=(n // tm, m // tn),
        in_specs=[
            pl.BlockSpec((tm, k), lambda i, j: (i, 0)),
            pl.BlockSpec((None, k, tn), lambda i, j: (layer, 0, j)),
            pl.BlockSpec((1, tn), lambda i, j: (0, jnp.minimum(j, n_norm_tiles - 1))),
        ],
        out_specs=[pl.BlockSpec((tm, tn), lambda i, j: (i, j))] + [pl.BlockSpec((tm, tn), lambda i, j: (i, 0))] * n_copies,
        out_shape=[jax.ShapeDtypeStruct((n, m), F32)] + [jax.ShapeDtypeStruct((n, tn), F32)] * n_copies,
        compiler_params=_params("parallel", "arbitrary"),
        name="proj",
    )(h, w_all, gains)
    return outs[0], outs[1:]


def _moba_prompt_kernel(q_ref, k_ref, v_ref, o_ref, kbf, vt, kmean, selbias, biased, s_even, s_odd, m_scr, acc_scr,
                        *, nblk, k_sel, scale):
    t = pl.program_id(2)
    n_tiles = pl.num_programs(2) - 1
    blk = MOBA_BLOCK
    tq = MOBA_QB * blk

    @pl.when(t == 0)
    def _():
        for j in range(nblk):
            kj = k_ref[j * blk:(j + 1) * blk, :]
            kbf[j * blk:(j + 1) * blk, :] = kj.astype(BF16)
            kmean[j:j + 1, :] = jnp.mean(kj, axis=0, keepdims=True)
        for jp in range(nblk // MOBA_QB):
            vt[jp] = v_ref[jp * tq:(jp + 1) * tq, :].T.astype(BF16)
        kpos = lax.broadcasted_iota(jnp.int32, (tq, tq), 0)
        qpos = lax.broadcasted_iota(jnp.int32, (tq, tq), 1)
        biased[...] = jnp.where((kpos // blk == qpos // blk) & (kpos <= qpos), 0.0, 1.0)

    acc_scr[...] = jnp.zeros(acc_scr.shape, F32)

    @pl.when(t < n_tiles)
    def _():
        sc = lax.dot_general(kmean[...], q_ref[...], _NT, precision=lax.Precision.HIGHEST,
                             preferred_element_type=F32)
        jj = lax.broadcasted_iota(jnp.int32, sc.shape, 0)
        own = MOBA_QB * t + lax.broadcasted_iota(jnp.int32, sc.shape, 1) // blk
        past = jj < own
        sc = jnp.where(past, sc, MASK_FILL)
        rank = jnp.zeros(sc.shape, jnp.int32)
        for j2 in range(nblk):
            sj = sc[j2:j2 + 1, :]
            beats = (sj > sc) | ((sj == sc) & (j2 < jj))
            rank = rank + beats.astype(jnp.int32)
        selbias[...] = jnp.where(past & (rank < k_sel), 0.0, MASK_FILL)

    def run(s_cur, s_prev, cur):
        def finish_group(jp, l):
            p = jnp.exp2(s_prev[jp] - m_scr[1 - cur])
            acc_scr[...] += jnp.dot(vt[jp], p.astype(BF16), preferred_element_type=F32)
            return l + jnp.sum(p, axis=0, keepdims=True)

        def write_out(l):
            @pl.when(t > 0)
            def _():
                o_ref[...] = (acc_scr[...] / l).T.astype(o_ref.dtype)

        @pl.when(t < n_tiles)
        def _():
            qb = (q_ref[...] * (scale * LOG2E)).astype(BF16)

            def logits(jp):
                kj = kbf[pl.ds(pl.multiple_of(jp * tq, tq), tq), :]
                return lax.dot_general(kj, qb, _NT, preferred_element_type=F32)

            def bias(jp):
                rows = [jnp.broadcast_to(selbias[pl.ds(jp * MOBA_QB + u, 1), :], (blk, tq))
                        for u in range(MOBA_QB)]
                return jnp.concatenate(rows, axis=0)

            def both(jp, carry):
                m, l = carry
                s = logits(jp) + bias(jp)
                s_cur[jp] = s
                return jnp.maximum(m, jnp.max(s, axis=0, keepdims=True)), finish_group(jp, l)

            m, l = lax.fori_loop(0, t, both, (jnp.full((1, tq), MASK_FILL, F32), jnp.zeros((1, tq), F32)))
            write_out(l)

            s = logits(t) + bias(t) * biased[...]
            s_cur[t] = s
            m_scr[cur] = jnp.maximum(m, jnp.max(s, axis=0, keepdims=True))

        @pl.when(t == n_tiles)
        def _():
            write_out(lax.fori_loop(0, t, finish_group, jnp.zeros((1, tq), F32)))

    @pl.when(t % 2 == 0)
    def _():
        run(s_even, s_odd, 0)

    @pl.when(t % 2 == 1)
    def _():
        run(s_odd, s_even, 1)


def _moba_prompt(p3, n_heads):
    b, l, _ = p3.shape
    tq = MOBA_QB * MOBA_BLOCK
    assert l % tq == 0
    nblk = l // MOBA_BLOCK
    n_tiles = l // tq
    dh = HEAD_DIM
    kern = functools.partial(_moba_prompt_kernel, nblk=nblk, k_sel=min(MOBA_TOPK, nblk - 1), scale=dh ** -0.5)
    return pl.pallas_call(
        kern,
        grid=(b, n_heads, n_tiles + 1),
        in_specs=[
            pl.BlockSpec((None, tq, dh), lambda bb, h, i: (bb, jnp.minimum(i, n_tiles - 1), h)),
            pl.BlockSpec((None, l, dh), lambda bb, h, i: (bb, 0, n_heads + h)),
            pl.BlockSpec((None, l, dh), lambda bb, h, i: (bb, 0, 2 * n_heads + h)),
        ],
        out_specs=pl.BlockSpec((None, tq, dh), lambda bb, h, i: (bb, jnp.maximum(i - 1, 0), h)),
        out_shape=jax.ShapeDtypeStruct((b, l, n_heads * dh), BF16),
        scratch_shapes=[
            pltpu.VMEM((l, dh), BF16),
            pltpu.VMEM((n_tiles, dh, tq), BF16),
            pltpu.VMEM((nblk, dh), F32),
            pltpu.VMEM((nblk, tq), F32),
            pltpu.VMEM((tq, tq), F32),
            pltpu.VMEM((n_tiles, tq, tq), F32),
            pltpu.VMEM((n_tiles, tq, tq), F32),
            pltpu.VMEM((2, 1, tq), F32),
            pltpu.VMEM((dh, tq), F32),
        ],
        compiler_params=_params("parallel", "parallel", "arbitrary"),
        name="moba_prompt",
    )(p3, p3, p3)


def _moba_sample_kernel(pt_ref, q_ref, kn_ref, vn_ref, *rest, n_steps, n_heads, n_new, page, k_sel, scale):
    g = PAGES_PER_STEP
    k_refs = rest[:g]
    v_refs = rest[g:2 * g]
    o_ref = rest[2 * g]
    s_all, ksum, selb, headb, newpad, acc, lacc, m_scr, lnew = rest[2 * g + 1:]
    ph = pl.program_id(1)
    st = pl.program_id(2)
    dh = HEAD_DIM
    rows = n_new * n_heads
    lanes = page * n_heads
    ppb = MOBA_BLOCK // page
    n_pages = n_steps * g
    n_past = n_pages // ppb
    vw = selb.shape[2]

    def same_head(shape):
        r = lax.broadcasted_iota(jnp.int32, shape, 0)
        c = lax.broadcasted_iota(jnp.int32, shape, 1)
        return (r // n_new) == (c % n_heads)

    def qscaled():
        return (q_ref[...] * (scale * LOG2E)).astype(BF16)

    def page_bias(pidx):
        return jnp.concatenate([selb[pidx // ppb]] * (lanes // vw), axis=1)

    @pl.when(ph == 0)
    def _():
        qb16 = qscaled()
        for u in range(g):
            kp = k_refs[u][...]
            s_all[st * g + u] = lax.dot_general(qb16, kp.astype(BF16), _NT, preferred_element_type=F32)
            srow = jnp.sum(kp.reshape(page, n_heads, dh), axis=0)
            dst = pl.ds(pl.multiple_of((st * (g // ppb) + u // ppb) * n_heads, n_heads), n_heads)
            if u % ppb == 0:
                ksum[dst, :] = srow
            else:
                ksum[dst, :] = ksum[dst, :] + srow

    @pl.when((ph == 1) & (st == 0))
    def _():
        kmean = ksum[...] * (1.0 / MOBA_BLOCK)
        sc2 = lax.dot_general(q_ref[...], kmean, _NT, precision=lax.Precision.HIGHEST,
                              preferred_element_type=F32)
        sc2 = jnp.where(same_head(sc2.shape), sc2, 0.0)
        gr = lax.broadcasted_iota(jnp.int32, (n_past * n_heads, n_past), 0)
        gc = lax.broadcasted_iota(jnp.int32, (n_past * n_heads, n_past), 1)
        pick = jnp.where(gr // n_heads == gc, 1.0, 0.0)
        sc = jnp.dot(sc2, pick, precision=lax.Precision.HIGHEST, preferred_element_type=F32)
        jj = lax.broadcasted_iota(jnp.int32, sc.shape, 1)
        rank = jnp.zeros(sc.shape, jnp.int32)
        for j2 in range(n_past):
            sj = sc[:, j2:j2 + 1]
            beats = (sj > sc) | ((sj == sc) & (j2 < jj))
            rank = rank + beats.astype(jnp.int32)
        sel = jnp.where(rank < k_sel, 0.0, MASK_FILL)
        for j2 in range(n_past):
            selb[j2] = jnp.broadcast_to(sel[:, j2:j2 + 1], (rows, vw))
        headb[...] = jnp.where(same_head((rows, lanes)), 0.0, MASK_FILL)

        qb16 = qscaled()
        newpad[...] = jnp.zeros(newpad.shape, F32)
        newpad[0:rows, :] = kn_ref[...]
        s_new = lax.dot_general(qb16, newpad[...].astype(BF16), _NT, preferred_element_type=F32)
        r = lax.broadcasted_iota(jnp.int32, s_new.shape, 0)
        c = lax.broadcasted_iota(jnp.int32, s_new.shape, 1)
        s_new = jnp.where(same_head(s_new.shape) & (c // n_heads <= r % n_new), s_new, MASK_FILL)

        def running_max(pidx, mx):
            return jnp.maximum(mx, s_all[pidx] + page_bias(pidx))

        mx = lax.fori_loop(0, n_pages, running_max, jnp.full((rows, lanes), 2 * MASK_FILL, F32))
        m = jnp.maximum(jnp.max(mx + headb[...], axis=-1, keepdims=True), jnp.max(s_new, axis=-1, keepdims=True))
        m_scr[...] = m
        p_new = jnp.exp2(s_new - m)
        lnew[...] = jnp.sum(p_new, axis=-1, keepdims=True)
        newpad[...] = jnp.zeros(newpad.shape, F32)
        newpad[0:rows, :] = vn_ref[...]
        acc[...] = jnp.dot(p_new.astype(BF16), newpad[...].astype(BF16), preferred_element_type=F32)
        lacc[...] = jnp.zeros(lacc.shape, F32)

    @pl.when(ph == 1)
    def _():
        a = acc[...]
        ls = lacc[...]
        off = headb[...] - m_scr[...]
        for u in range(g):
            pidx = st * g + u
            p = jnp.exp2(s_all[pidx] + page_bias(pidx) + off)
            ls = ls + p
            a = a + jnp.dot(p.astype(BF16), v_refs[u][...].astype(BF16), preferred_element_type=F32)
        acc[...] = a
        lacc[...] = ls

    @pl.when((ph == 1) & (st == n_steps - 1))
    def _():
        o_ref[...] = acc[...] / (jnp.sum(lacc[...], axis=-1, keepdims=True) + lnew[...])


def _moba_sample(p3, cache_k, cache_v, layer, page_table, n_heads):
    b, n_new, _ = p3.shape
    dh = HEAD_DIM
    width = n_heads * dh
    lanes = cache_k.shape[2]
    page = lanes // n_heads
    n_pages = page_table.shape[1]
    g = PAGES_PER_STEP
    ppb = MOBA_BLOCK // page
    rows = n_new * n_heads
    vw = 128
    assert MOBA_BLOCK % page == 0 and g % ppb == 0 and n_pages % g == 0 and rows <= vw and lanes % vw == 0
    n_steps = n_pages // g
    n_past = n_pages // ppb
    kern = functools.partial(_moba_sample_kernel, n_steps=n_steps, n_heads=n_heads, n_new=n_new, page=page,
                             k_sel=min(MOBA_TOPK, n_past), scale=dh ** -0.5)

    heads = lambda c0: p3[:, :, c0:c0 + width].reshape(b, n_new, n_heads, dh)
    q_rows = heads(0).transpose(0, 2, 1, 3).reshape(b, rows, dh)
    kn_rows = heads(width).reshape(b, rows, dh)
    vn_rows = heads(2 * width).reshape(b, rows, dh)

    def k_map(u):
        return lambda bb, ph, st, pt: (layer, pt[bb, jnp.where(ph == 0, st, n_steps - 1) * g + u], 0, 0)

    def v_map(u):
        return lambda bb, ph, st, pt: (layer, pt[bb, jnp.where(ph == 0, 0, st) * g + u], 0, 0)

    new_spec = pl.BlockSpec((None, rows, dh), lambda bb, ph, st, pt: (bb, 0, 0))
    page_spec = lambda imap: pl.BlockSpec((None, None, lanes, dh), imap)
    grid_spec = pltpu.PrefetchScalarGridSpec(
        num_scalar_prefetch=1,
        grid=(b, 2, n_steps),
        in_specs=[new_spec, new_spec, new_spec]
                 + [page_spec(k_map(u)) for u in range(g)] + [page_spec(v_map(u)) for u in range(g)],
        out_specs=new_spec,
        scratch_shapes=[
            pltpu.VMEM((n_pages, rows, lanes), F32),
            pltpu.VMEM((n_past * n_heads, dh), F32),
            pltpu.VMEM((n_past, rows, vw), F32),
            pltpu.VMEM((rows, lanes), F32),
            pltpu.VMEM((vw, dh), F32),
            pltpu.VMEM((rows, dh), F32),
            pltpu.VMEM((rows, lanes), F32),
            pltpu.VMEM((rows, 1), F32),
            pltpu.VMEM((rows, 1), F32),
        ],
    )
    o = pl.pallas_call(
        kern,
        grid_spec=grid_spec,
        out_shape=jax.ShapeDtypeStruct((b, rows, dh), F32),
        compiler_params=_params("parallel", "arbitrary", "arbitrary"),
        name="moba_sample",
    )(page_table, q_rows, kn_rows, vn_rows, *([cache_k] * g), *([cache_v] * g))
    return o.reshape(b, n_heads, n_new, dh).transpose(0, 2, 1, 3).reshape(b, n_new, width)


def _hgrn_levels():
    sizes, h = [], HG_SUB
    while h < HG_CHUNK:
        sizes.append(h)
        h *= 2
    return sizes


def _hgrn_prompt_kernel(xq_ref, xf_ref, xi_ref, xg_ref, lb_ref, gn_ref, s0_ref, o_ref, s_ref,
                        st_scr, st_old, tril_scr, group_scr, diag_scr):
    t = pl.program_id(2)
    n = xq_ref.shape[0]
    dh = HEAD_DIM
    c = HG_CHUNK
    levels = _hgrn_levels()

    @pl.when(t == 0)
    def _():
        for hh in range(HG_HEADS_PER_STEP):
            st_scr[hh] = s0_ref[hh].T
        r = lax.broadcasted_iota(jnp.int32, (n, n), 0)
        cc = lax.broadcasted_iota(jnp.int32, (n, n), 1)
        tril_scr[...] = jnp.where((cc <= r) & (r // c == cc // c), 1.0, 0.0).astype(BF16)
        diag_scr[...] = jnp.where((cc <= r) & (r // HG_SUB == cc // HG_SUB), 1.0, 0.0)
        for li, hs in enumerate(levels):
            group_scr[li] = jnp.where(r // (2 * hs) == cc // (2 * hs), 1.0, 0.0)

    heads = range(HG_HEADS_PER_STEP)
    cols = [slice(hh * dh, (hh + 1) * dh) for hh in heads]

    def finish(hh, o):
        y = o * lax.rsqrt(jnp.mean(o * o, axis=-1, keepdims=True) + EPS) * gn_ref[...]
        o_ref[:, cols[hh]] = (y * _sigmoid(xg_ref[:, cols[hh]])).astype(o_ref.dtype)

    qkg = [_hgrn_gates(xq_ref[:, cs], xf_ref[:, cs], lb_ref[:, cs]) for cs in cols]
    vb = [xi_ref[:, cs].astype(BF16) for cs in cols]
    b = [_hgrn_cumsum(g2, tril_scr) for _, _, g2 in qkg]
    grow = [_hgrn_tile_grow(bh) for bh in b]
    a = [_hgrn_level_weights(q, k, bh, group_scr) for (q, k, _), bh in zip(qkg, b)]
    ad = [_hgrn_tile_weights(q, k, jnp.minimum(gh, HG_MAX_GROW), diag_scr) for (q, k, _), gh in zip(qkg, grow)]
    o = [jnp.dot((ah + adh).astype(BF16), vh, preferred_element_type=F32) for ah, adh, vh in zip(a, ad, vb)]
    for hh in heads:
        st_old[hh] = st_scr[hh]
        oh, st = _hgrn_carry(qkg[hh][0], qkg[hh][1], b[hh], vb[hh], o[hh], st_scr[hh])
        st_scr[hh] = st
        finish(hh, oh)

    for hh in heads:
        @pl.when(jnp.max(grow[hh]) > HG_MAX_GROW)
        def _(hh=hh):
            cs = cols[hh]
            q, k, g2 = _hgrn_gates(xq_ref[:, cs], xf_ref[:, cs], lb_ref[:, cs])
            v = xi_ref[:, cs]
            bh = _hgrn_cumsum(g2, tril_scr)
            o_lv = jnp.dot(_hgrn_level_weights(q, k, bh, group_scr).astype(BF16), v.astype(BF16),
                           preferred_element_type=F32)
            oh, _ = _hgrn_carry(q, k, bh, v.astype(BF16), o_lv + _hgrn_tile_pairs(q, k, bh, v), st_old[hh])
            finish(hh, oh)

    @pl.when(t == pl.num_programs(2) - 1)
    def _():
        for hh in heads:
            s_ref[hh] = st_scr[hh].T


def _hgrn_gates(xq, xf, lb):
    f = lb + (1.0 - lb) * _sigmoid(xf)
    return xq * _sigmoid(xq), 1.0 - f, jnp.log2(f)


def _hgrn_cumsum(g2, tril_scr):
    b = None
    resid = g2
    for _ in range(3):
        part = resid.astype(BF16)
        resid = resid - part.astype(F32)
        term = jnp.dot(tril_scr[...], part, preferred_element_type=F32)
        b = term if b is None else b + term
    return b


def _hgrn_level_weights(q, k, b, group_scr):
    n, dh = q.shape
    a = None
    for li, hs in enumerate(_hgrn_levels()):
        zero = jnp.zeros((hs, dh), F32)
        q_rows, k_rows = [], []
        for g0 in range(0, n, 2 * hs):
            first, second = slice(g0, g0 + hs), slice(g0 + hs, g0 + 2 * hs)
            bref = b[g0 + hs - 1:g0 + hs, :]
            q_rows += [zero, q[second] * jnp.exp2(b[second] - bref)]
            k_rows += [k[first] * jnp.exp2(bref - b[first]), zero]
        ql = jnp.concatenate(q_rows, axis=0).astype(BF16)
        kl = jnp.concatenate(k_rows, axis=0).astype(BF16)
        al = lax.dot_general(ql, kl, _NT, preferred_element_type=F32) * group_scr[li]
        a = al if a is None else a + al
    return a


def _hgrn_tile_grow(b):
    n, dh = b.shape
    zero_row = jnp.zeros((1, dh), F32)
    b0 = jnp.concatenate([jnp.broadcast_to(zero_row if r0 % HG_CHUNK == 0 else b[r0 - 1:r0, :], (HG_SUB, dh))
                          for r0 in range(0, n, HG_SUB)], axis=0)
    return b0 - b


def _hgrn_tile_weights(q, k, grow, diag_scr):
    qd = (q * jnp.exp2(-grow)).astype(BF16)
    kd = (k * jnp.exp2(grow)).astype(BF16)
    return lax.dot_general(qd, kd, _NT, preferred_element_type=F32) * diag_scr[...]


def _hgrn_tile_pairs(q, k, b, v):
    n, dh = q.shape
    sub = HG_SUB
    q3, k3, b3, v3 = (x.reshape(n // sub, sub, dh) for x in (q, k, b, v))
    tio = lax.broadcasted_iota(jnp.int32, q3.shape, 1)
    od = jnp.zeros(q3.shape, F32)
    for s in range(sub):
        e = jnp.exp2(jnp.where(tio >= s, b3 - b3[:, s:s + 1, :], MASK_FILL))
        w = q3 * k3[:, s:s + 1, :] * e
        od = od + jnp.sum(w, axis=-1, keepdims=True) * v3[:, s:s + 1, :]
    return od.reshape(n, dh)


def _hgrn_carry(q, k, b, vb, o, st):
    n = q.shape[0]
    c = HG_CHUNK
    parts = []
    for c0 in range(0, n, c):
        rows = slice(c0, c0 + c)
        bc = b[rows]
        b_end = bc[c - 1:c, :]
        qd = (q[rows] * jnp.exp2(bc)).astype(BF16)
        parts.append(o[rows] + lax.dot_general(qd, st.astype(BF16), _NT, preferred_element_type=F32))
        kd = (k[rows] * jnp.exp2(b_end - bc)).astype(BF16)
        st = st * jnp.exp2(b_end) + lax.dot_general(vb[rows], kd, _TN, preferred_element_type=F32)
    return jnp.concatenate(parts, axis=0), st


def _hgrn_prompt(p3, lb, gn, s0, n_heads, col0):
    b, l, _ = p3.shape
    dh = HEAD_DIM
    step = min(HG_STEP, l)
    assert l % step == 0 and step % HG_CHUNK == 0
    hps = HG_HEADS_PER_STEP
    assert n_heads % hps == 0 and col0 % (hps * dh) == 0
    c0 = col0 // (hps * dh)
    spec = lambda off: pl.BlockSpec((None, step, hps * dh), lambda bb, h, t: (bb, t, c0 + off * (n_heads // hps) + h))
    vec = pl.BlockSpec((1, hps * dh), lambda bb, h, t: (0, h))
    state = pl.BlockSpec((None, hps, dh, dh), lambda bb, h, t: (bb, h, 0, 0))
    return pl.pallas_call(
        _hgrn_prompt_kernel,
        grid=(b, n_heads // hps, l // step),
        in_specs=[spec(0), spec(1), spec(2), spec(3), vec, pl.BlockSpec((1, dh), lambda bb, h, t: (0, 0)), state],
        out_specs=[pl.BlockSpec((None, step, hps * dh), lambda bb, h, t: (bb, t, h)), state],
        out_shape=[jax.ShapeDtypeStruct((b, l, n_heads * dh), BF16), jax.ShapeDtypeStruct(s0.shape, F32)],
        scratch_shapes=[pltpu.VMEM((hps, dh, dh), F32),
                        pltpu.VMEM((hps, dh, dh), F32),
                        pltpu.VMEM((step, step), BF16),
                        pltpu.VMEM((len(_hgrn_levels()), step, step), F32),
                        pltpu.VMEM((step, step), F32)],
        compiler_params=_params("parallel", "parallel", "arbitrary"),
        name="hgrn_prompt",
    )(p3, p3, p3, p3, lb.reshape(1, -1), gn.reshape(1, dh), s0)


def _hgrn_sample_kernel(xq_ref, xf_ref, xi_ref, xg_ref, lb_ref, gn_ref, s0_ref, o_ref, s_ref):
    n_new = xq_ref.shape[0]
    dh = HEAD_DIM
    pad = jnp.zeros((8 - n_new % 8, dh), F32)
    cols = lambda a: jnp.concatenate([a, pad], axis=0).T
    for h in range(s0_ref.shape[0]):
        cs = slice(h * dh, (h + 1) * dh)
        xq = xq_ref[:, cs]
        f = lb_ref[:, cs] + (1.0 - lb_ref[:, cs]) * _sigmoid(xf_ref[:, cs])
        q = xq * _sigmoid(xq)
        f_c, k_c, q_c = cols(f), cols(1.0 - f), cols(q)
        v = xi_ref[:, cs]
        s = s0_ref[h]
        outs = []
        for t in range(n_new):
            s = f_c[:, t:t + 1] * s + k_c[:, t:t + 1] * v[t:t + 1, :]
            outs.append(jnp.sum(q_c[:, t:t + 1] * s, axis=0, keepdims=True))
        o = jnp.concatenate(outs, axis=0)
        y = o * lax.rsqrt(jnp.mean(o * o, axis=-1, keepdims=True) + EPS) * gn_ref[...]
        o_ref[:, cs] = y * _sigmoid(xg_ref[:, cs])
        s_ref[h] = s


def _hgrn_sample(p3, lb, gn, state_all, layer, n_heads, col0):
    b, n_new, _ = p3.shape
    dh = HEAD_DIM
    width = n_heads * dh
    assert col0 % width == 0
    spec = lambda off: pl.BlockSpec((None, n_new, width), lambda bb: (bb, 0, col0 // width + off))
    return pl.pallas_call(
        _hgrn_sample_kernel,
        grid=(b,),
        in_specs=[spec(0), spec(1), spec(2), spec(3),
                  pl.BlockSpec((1, width), lambda bb: (0, 0)), pl.BlockSpec((1, dh), lambda bb: (0, 0)),
                  pl.BlockSpec((None, None, n_heads, dh, dh), lambda bb: (layer, bb, 0, 0, 0))],
        out_specs=[pl.BlockSpec((None, n_new, width), lambda bb: (bb, 0, 0)),
                   pl.BlockSpec((None, n_heads, dh, dh), lambda bb: (bb, 0, 0, 0))],
        out_shape=[jax.ShapeDtypeStruct((b, n_new, width), F32),
                   jax.ShapeDtypeStruct(state_all.shape[1:], F32)],
        compiler_params=_params("parallel"),
        name="hgrn_sample",
    )(p3, p3, p3, p3, lb.reshape(1, -1), gn.reshape(1, dh), state_all)


def _mem_attn_kernel(q_ref, k_ref, v_ref, g_ref, o_ref, *, scale):
    dm = g_ref.shape[1]
    for h in range(q_ref.shape[1] // dm):
        cols = slice(h * dm, (h + 1) * dm)
        q = q_ref[:, cols]
        qn = q * lax.rsqrt(jnp.mean(q * q, axis=-1, keepdims=True) + EPS) * g_ref[...]
        s = lax.dot_general(qn.astype(BF16), k_ref[:, cols].astype(BF16), _NT, preferred_element_type=F32) * scale
        p = jnp.exp(s - jnp.max(s, axis=-1, keepdims=True))
        o = jnp.dot(p.astype(BF16), v_ref[:, cols].astype(BF16), preferred_element_type=F32)
        o_ref[:, cols] = (o / jnp.sum(p, axis=-1, keepdims=True)).astype(o_ref.dtype)


def _mem_attn(p3, mem_k, mem_v, layer, gn, col0, tq, out_dtype):
    b, l, _ = p3.shape
    mtok, width = mem_k.shape[2], mem_k.shape[3]
    dm = width // MEM_HEADS
    kern = functools.partial(_mem_attn_kernel, scale=dm ** -0.5)
    kv = pl.BlockSpec((None, None, mtok, width), lambda bb, i: (layer, bb, 0, 0))
    return pl.pallas_call(
        kern,
        grid=(b, l // tq),
        in_specs=[pl.BlockSpec((None, tq, width), lambda bb, i: (bb, i, col0 // width)), kv, kv,
                  pl.BlockSpec((1, dm), lambda bb, i: (0, 0))],
        out_specs=pl.BlockSpec((None, tq, width), lambda bb, i: (bb, i, 0)),
        out_shape=jax.ShapeDtypeStruct((b, l, width), out_dtype),
        compiler_params=_params("parallel", "arbitrary"),
        name="mem_attn",
    )(p3, mem_k, mem_v, gn.reshape(1, dm))


def _merge_kernel(oa_ref, oh_ref, om_ref, ga_ref, gh_ref, gm_ref, wa_ref, wh_ref, wm_ref, o_ref):
    acc = _sigmoid(ga_ref[...]) * jnp.dot(oa_ref[...].astype(BF16), wa_ref[...], preferred_element_type=F32)
    acc = acc + _sigmoid(gh_ref[...]) * jnp.dot(oh_ref[...].astype(BF16), wh_ref[...], preferred_element_type=F32)
    acc = acc + _sigmoid(gm_ref[...]) * jnp.dot(om_ref[...].astype(BF16), wm_ref[...], preferred_element_type=F32)
    o_ref[...] = acc.astype(o_ref.dtype)


def _merge(o_a, o_h, o_m, p2, gate_col0, w_branch, layer, tm, tn):
    n, bw = o_a.shape
    d = w_branch.shape[3]
    gspec = lambda br: pl.BlockSpec((tm, tn), lambda i, j: (i, (gate_col0 + br * d) // tn + j))
    wspec = lambda br: pl.BlockSpec((None, None, bw, tn), lambda i, j: (layer, br, 0, j))
    ospec = pl.BlockSpec((tm, bw), lambda i, j: (i, 0))
    return pl.pallas_call(
        _merge_kernel,
        grid=(n // tm, d // tn),
        in_specs=[ospec, ospec, ospec, gspec(0), gspec(1), gspec(2), wspec(0), wspec(1), wspec(2)],
        out_specs=pl.BlockSpec((tm, tn), lambda i, j: (i, j)),
        out_shape=jax.ShapeDtypeStruct((n, d), BF16),
        compiler_params=_params("parallel", "arbitrary"),
        name="merge",
    )(o_a, o_h, o_m, p2, p2, p2, w_branch, w_branch, w_branch)


def _out_proj_kernel(m_ref, w_ref, x_ref, g_ref, y_ref, h_ref):
    y = x_ref[...] + jnp.dot(m_ref[...], w_ref[...], preferred_element_type=F32)
    y_ref[...] = y
    hn = y * lax.rsqrt(jnp.mean(y * y, axis=-1, keepdims=True) + EPS)
    h_ref[...] = (hn * g_ref[...]).astype(h_ref.dtype)


def _out_proj(m, w_out, layer, x, g, tm):
    n, d = x.shape
    row = pl.BlockSpec((tm, d), lambda i: (i, 0))
    return pl.pallas_call(
        _out_proj_kernel,
        grid=(n // tm,),
        in_specs=[row, pl.BlockSpec((None, d, d), lambda i: (layer, 0, 0)), row, pl.BlockSpec((1, d), lambda i: (0, 0))],
        out_specs=[row, row],
        out_shape=[jax.ShapeDtypeStruct((n, d), F32), jax.ShapeDtypeStruct((n, d), BF16)],
        compiler_params=_params("parallel"),
        name="out_proj",
    )(m, w_out, x, g.reshape(1, d))


def _mlp_kernel(h_ref, wu_ref, wd_ref, x_ref, g_ref, y_ref, hn_ref, acc):
    f = pl.program_id(1)

    @pl.when(f == 0)
    def _():
        acc[...] = jnp.zeros(acc.shape, F32)

    u = jnp.maximum(jnp.dot(h_ref[...], wu_ref[...], preferred_element_type=F32), 0.0)
    acc[...] += jnp.dot((u * u).astype(BF16), wd_ref[...], preferred_element_type=F32)

    @pl.when(f == pl.num_programs(1) - 1)
    def _():
        y = x_ref[...] + acc[...]
        y_ref[...] = y
        hn = y * lax.rsqrt(jnp.mean(y * y, axis=-1, keepdims=True) + EPS)
        hn_ref[...] = (hn * g_ref[...]).astype(hn_ref.dtype)


def _mlp(h, w_up, w_down, layer, x, g_next, tm, tf):
    n, d = x.shape
    dff = w_up.shape[2]
    row = pl.BlockSpec((tm, d), lambda i, f: (i, 0))
    return pl.pallas_call(
        _mlp_kernel,
        grid=(n // tm, dff // tf),
        in_specs=[row, pl.BlockSpec((None, d, tf), lambda i, f: (layer, 0, f)),
                  pl.BlockSpec((None, tf, d), lambda i, f: (layer, f, 0)), row,
                  pl.BlockSpec((1, d), lambda i, f: (0, 0))],
        out_specs=[row, row],
        out_shape=[jax.ShapeDtypeStruct((n, d), F32), jax.ShapeDtypeStruct((n, d), BF16)],
        scratch_shapes=[pltpu.VMEM((tm, d), F32)],
        compiler_params=_params("parallel", "arbitrary"),
        name="mlp",
    )(h, w_up, w_down, x, g_next.reshape(1, d))


def _tiles(n):
    return (min(n, 1024), min(n, 1024), min(n, 256), min(n, 512))


def _trunk_layer(layer, x, h, shape, w, lb, next_gain, attn_fn, hgrn_fn, mem_fn):
    b, l = shape
    n, d = x.shape
    t_proj, t_merge, t_out, t_mlp = _tiles(n)
    n_heads = w["hg_width"] // HEAD_DIM
    width = n_heads * HEAD_DIM
    qk_gain = jnp.concatenate([jnp.tile(w["q_norm_moba"][layer], n_heads), jnp.tile(w["k_norm_moba"][layer], n_heads)])
    if n >= width:
        p2, (k_new, v_new) = _proj(h, w["w_in"], layer, qk_gain.reshape(1, -1), HEAD_DIM, t_proj, width, n_copies=2)
    else:
        p2, _ = _proj(h, w["w_in"], layer, qk_gain.reshape(1, -1), HEAD_DIM, t_proj, 2 * width)
        k_new, v_new = p2[:, width:2 * width], p2[:, 2 * width:3 * width]
    kv = lambda a: a.reshape(b, l, n_heads, HEAD_DIM)
    p3 = p2.reshape(b, l, -1)
    o_a = attn_fn(p3)
    o_h, s_new = hgrn_fn(p3, lb)
    o_m = mem_fn(p3)
    flat = lambda a: a.reshape(n, -1)
    m = _merge(flat(o_a), flat(o_h), flat(o_m), p2, w["gate_col0"], w["w_branch"], layer, t_merge, 512)
    x, h2 = _out_proj(m, w["w_out"], layer, x, w["ln_mlp"][layer], t_out)
    x, h_next = _mlp(h2, w["w_up"], w["w_down"], layer, x, next_gain, t_mlp, 1024)
    return x, h_next, kv(k_new), kv(v_new), s_new


def kernel(x_prompt, x_sample, mem_prompt, cache_moba_k, cache_moba_v, cache_mem_k, cache_mem_v, state_hgrn,
           page_table, ln_mix, w_in, q_norm_moba, k_norm_moba, lb_logits, hg_out_norm, q_norm_mem, ln_mem,
           w_mem_kv, k_norm_mem, w_branch, w_out, ln_mlp, w_up, w_down):
    depth = w_in.shape[0]
    bp, lp, d = x_prompt.shape
    bs, ls, _ = x_sample.shape
    hg_width = lb_logits.shape[1]
    n_heads = hg_width // HEAD_DIM
    moba_width = n_heads * HEAD_DIM
    mem_width = cache_mem_k.shape[3] * cache_mem_k.shape[4]
    mem_tokens = mem_prompt.shape[1]
    hg_col0 = 3 * moba_width
    mem_col0 = hg_col0 + 4 * hg_width
    gate_col0 = mem_col0 + mem_width

    w = dict(
        w_in=w_in, w_branch=w_branch.astype(BF16), w_out=w_out.astype(BF16),
        w_up=w_up.astype(BF16), w_down=w_down.astype(BF16), ln_mlp=ln_mlp,
        q_norm_moba=q_norm_moba, k_norm_moba=k_norm_moba, hg_width=hg_width, gate_col0=gate_col0,
    )
    lower = _lower_bounds(lb_logits)

    ck = cache_moba_k.reshape(cache_moba_k.shape[:2] + (-1, HEAD_DIM))
    cv = cache_moba_v.reshape(cache_moba_v.shape[:2] + (-1, HEAD_DIM))
    cmk = cache_mem_k.reshape(cache_mem_k.shape[:3] + (mem_width,))
    cmv = cache_mem_v.reshape(cache_mem_v.shape[:3] + (mem_width,))

    xp = x_prompt.reshape(bp * lp, d)
    xs = x_sample.reshape(bs * ls, d)
    hp = _rmsnorm_bf16(xp, ln_mix[0], min(bp * lp, 512))
    hs = _rmsnorm_bf16(xs, ln_mix[0], bs * ls)
    s0_p = jnp.zeros((bp,) + state_hgrn.shape[2:], F32)
    mem_flat = mem_prompt.reshape(bp * mem_tokens, d)

    kp_l, vp_l, mkp_l, mvp_l, sp_l, ks_l, vs_l, ss_l = [], [], [], [], [], [], [], []
    for layer in range(depth):
        next_gain = ln_mix[layer + 1] if layer + 1 < depth else ln_mix[layer]
        gn_h = hg_out_norm[layer]
        gn_m = q_norm_mem[layer]

        hm = _rmsnorm_bf16(mem_flat, ln_mem[layer], min(bp * mem_tokens, 512))
        mem_gain = jnp.tile(k_norm_mem[layer], MEM_HEADS).reshape(1, -1)
        mkv, _ = _proj(hm, w_mem_kv, layer, mem_gain, mem_width // MEM_HEADS, min(bp * mem_tokens, 512), mem_width)
        mk = mkv[:, :mem_width].reshape(1, bp, mem_tokens, mem_width)
        mv = mkv[:, mem_width:].reshape(1, bp, mem_tokens, mem_width)

        xp, hp, kp, vp, sp = _trunk_layer(
            layer, xp, hp, (bp, lp), w, lower[layer], next_gain,
            lambda p3: _moba_prompt(p3, n_heads),
            lambda p3, lb: _hgrn_prompt(p3, lb, gn_h, s0_p, n_heads, hg_col0),
            lambda p3: _mem_attn(p3, mk, mv, 0, gn_m, mem_col0, min(lp, 512), BF16))
        xs, hs, ks, vs, ss = _trunk_layer(
            layer, xs, hs, (bs, ls), w, lower[layer], next_gain,
            lambda p3: _moba_sample(p3, ck, cv, layer, page_table, n_heads),
            lambda p3, lb: _hgrn_sample(p3, lb, gn_h, state_hgrn, layer, n_heads, hg_col0),
            lambda p3: _mem_attn(p3, cmk, cmv, layer, gn_m, mem_col0, ls, F32))

        kp_l.append(kp)
        vp_l.append(vp)
        ks_l.append(ks)
        vs_l.append(vs)
        mem_shape = (bp, mem_tokens, MEM_HEADS, mem_width // MEM_HEADS)
        mkp_l.append(mk.reshape(mem_shape))
        mvp_l.append(mv.reshape(mem_shape))
        sp_l.append(sp)
        ss_l.append(ss)

    return (xp.reshape(bp, lp, d), xs.reshape(bs, ls, d), jnp.stack(kp_l), jnp.stack(vp_l), jnp.stack(mkp_l),
            jnp.stack(mvp_l), jnp.stack(sp_l), jnp.stack(ks_l), jnp.stack(vs_l), jnp.stack(ss_l))
```

```python
import functools

import jax
import jax.numpy as jnp
from jax import lax
from jax.experimental import pallas as pl
from jax.experimental.pallas import tpu as pltpu

F32 = jnp.float32
BF16 = jnp.bfloat16

EPS = 1e-6
MASK_FILL = -1e30
MOBA_BLOCK = 256
MOBA_TOPK = 3
HEAD_DIM = 128
MEM_HEADS = 4
MOBA_QB = 2
LOG2E = 1.4426950408889634
HG_CHUNK = 64
HG_SUB = 8
HG_STEP = 256
HG_HEADS_PER_STEP = 4
HG_MAX_GROW = 100.0
PAGES_PER_STEP = 16
VMEM_LIMIT_BYTES = 56 * 1024 * 1024

_NT = (((1,), (1,)), ((), ()))
_TN = (((0,), (0,)), ((), ()))


def _params(*sem):
    return pltpu.CompilerParams(dimension_semantics=sem, vmem_limit_bytes=VMEM_LIMIT_BYTES)


def _sigmoid(x):
    return 1.0 / (1.0 + jnp.exp(-x))


def _lower_bounds_kernel(lg_ref, out_ref):
    x = lg_ref[...]
    e = jnp.exp(x - jnp.max(x, axis=0, keepdims=True))
    p = e / jnp.sum(e, axis=0, keepdims=True)
    c = p[0:1]
    rows = [c - p[0:1]]
    for l in range(1, x.shape[0]):
        c = c + p[l:l + 1]
        rows.append(c - p[0:1])
    out_ref[...] = jnp.concatenate(rows, axis=0)


def _lower_bounds(lb_logits):
    return pl.pallas_call(
        _lower_bounds_kernel,
        out_shape=jax.ShapeDtypeStruct(lb_logits.shape, F32),
        name="lower_bounds",
    )(lb_logits)


def _rmsnorm_kernel(x_ref, g_ref, o_ref):
    x = x_ref[...]
    y = x * lax.rsqrt(jnp.mean(x * x, axis=-1, keepdims=True) + EPS)
    o_ref[...] = (y * g_ref[...]).astype(o_ref.dtype)


def _rmsnorm_bf16(x, g, tm):
    n, d = x.shape
    return pl.pallas_call(
        _rmsnorm_kernel,
        grid=(n // tm,),
        in_specs=[pl.BlockSpec((tm, d), lambda i: (i, 0)), pl.BlockSpec((1, d), lambda i: (0, 0))],
        out_specs=pl.BlockSpec((tm, d), lambda i: (i, 0)),
        out_shape=jax.ShapeDtypeStruct((n, d), BF16),
        compiler_params=_params("parallel"),
        name="rmsnorm",
    )(x, g.reshape(1, d))


def _proj_kernel(h_ref, w_ref, g_ref, o_ref, *copies, n_norm_tiles, group):
    o_ref[...] = jnp.dot(h_ref[...], w_ref[...], preferred_element_type=F32)
    j = pl.program_id(1)

    @pl.when(j < n_norm_tiles)
    def _():
        for c in range(o_ref.shape[1] // group):
            a = o_ref[:, c * group:(c + 1) * group]
            y = a * lax.rsqrt(jnp.mean(a * a, axis=-1, keepdims=True) + EPS)
            o_ref[:, c * group:(c + 1) * group] = y * g_ref[:, c * group:(c + 1) * group]

    for c, copy_ref in enumerate(copies):
        @pl.when(j == c + 1)
        def _(copy_ref=copy_ref):
            copy_ref[...] = o_ref[...]


def _proj(h, w_all, layer, gains, group, tm, tn, n_copies=0):
    n, k = h.shape
    m = w_all.shape[2]
    n_norm_tiles = gains.shape[1] // tn
    kern = functools.partial(_proj_kernel, n_norm_tiles=n_norm_tiles, group=group)
    outs = pl.pallas_call(
        kern,
        grid=(n // tm, m // tn),
        in_specs=[
            pl.BlockSpec((tm, k), lambda i, j: (i, 0)),
            pl.BlockSpec((None, k, tn), lambda i, j: (layer, 0, j)),
            pl.BlockSpec((1, tn), lambda i, j: (0, jnp.minimum(j, n_norm_tiles - 1))),
        ],
        out_specs=[pl.BlockSpec((tm, tn), lambda i, j: (i, j))] + [pl.BlockSpec((tm, tn), lambda i, j: (i, 0))] * n_copies,
        out_shape=[jax.ShapeDtypeStruct((n, m), F32)] + [jax.ShapeDtypeStruct((n, tn), F32)] * n_copies,
        compiler_params=_params("parallel", "arbitrary"),
        name="proj",
    )(h, w_all, gains)
    return outs[0], outs[1:]


def _moba_prompt_kernel(q_ref, k_ref, v_ref, o_ref, kbf, vt, kmean, selbias, biased, s_even, s_odd, m_scr, acc_scr,
                        *, nblk, k_sel, scale):
    t = pl.program_id(2)
    n_tiles = pl.num_programs(2) - 1
    blk = MOBA_BLOCK
    tq = MOBA_QB * blk

    @pl.when(t == 0)
    def _():
        for j in range(nblk):
            kj = k_ref[j * blk:(j + 1) * blk, :]
            kbf[j * blk:(j + 1) * blk, :] = kj.astype(BF16)
            kmean[j:j + 1, :] = jnp.mean(kj, axis=0, keepdims=True)
        for jp in range(nblk // MOBA_QB):
            vt[jp] = v_ref[jp * tq:(jp + 1) * tq, :].T.astype(BF16)
        kpos = lax.broadcasted_iota(jnp.int32, (tq, tq), 0)
        qpos = lax.broadcasted_iota(jnp.int32, (tq, tq), 1)
        biased[...] = jnp.where((kpos // blk == qpos // blk) & (kpos <= qpos), 0.0, 1.0)

    acc_scr[...] = jnp.zeros(acc_scr.shape, F32)

    @pl.when(t < n_tiles)
    def _():
        sc = lax.dot_general(kmean[...], q_ref[...], _NT, precision=lax.Precision.HIGHEST,
                             preferred_element_type=F32)
        jj = lax.broadcasted_iota(jnp.int32, sc.shape, 0)
        own = MOBA_QB * t + lax.broadcasted_iota(jnp.int32, sc.shape, 1) // blk
        past = jj < own
        sc = jnp.where(past, sc, MASK_FILL)
        rank = jnp.zeros(sc.shape, jnp.int32)
        for j2 in range(nblk):
            sj = sc[j2:j2 + 1, :]
            beats = (sj > sc) | ((sj == sc) & (j2 < jj))
            rank = rank + beats.astype(jnp.int32)
        selbias[...] = jnp.where(past & (rank < k_sel), 0.0, MASK_FILL)

    def run(s_cur, s_prev, cur):
        def finish_group(jp, l):
            p = jnp.exp2(s_prev[jp] - m_scr[1 - cur])
            acc_scr[...] += jnp.dot(vt[jp], p.astype(BF16), preferred_element_type=F32)
            return l + jnp.sum(p, axis=0, keepdims=True)

        def write_out(l):
            @pl.when(t > 0)
            def _():
                o_ref[...] = (acc_scr[...] / l).T.astype(o_ref.dtype)

        @pl.when(t < n_tiles)
        def _():
            qb = (q_ref[...] * (scale * LOG2E)).astype(BF16)

            def logits(jp):
                kj = kbf[pl.ds(pl.multiple_of(jp * tq, tq), tq), :]
                return lax.dot_general(kj, qb, _NT, preferred_element_type=F32)

            def bias(jp):
                rows = [jnp.broadcast_to(selbias[pl.ds(jp * MOBA_QB + u, 1), :], (blk, tq))
                        for u in range(MOBA_QB)]
                return jnp.concatenate(rows, axis=0)

            def both(jp, carry):
                m, l = carry
                s = logits(jp) + bias(jp)
                s_cur[jp] = s
                return jnp.maximum(m, jnp.max(s, axis=0, keepdims=True)), finish_group(jp, l)

            m, l = lax.fori_loop(0, t, both, (jnp.full((1, tq), MASK_FILL, F32), jnp.zeros((1, tq), F32)))
            write_out(l)

            s = logits(t) + bias(t) * biased[...]
            s_cur[t] = s
            m_scr[cur] = jnp.maximum(m, jnp.max(s, axis=0, keepdims=True))

        @pl.when(t == n_tiles)
        def _():
            write_out(lax.fori_loop(0, t, finish_group, jnp.zeros((1, tq), F32)))

    @pl.when(t % 2 == 0)
    def _():
        run(s_even, s_odd, 0)

    @pl.when(t % 2 == 1)
    def _():
        run(s_odd, s_even, 1)


def _moba_prompt(p3, n_heads):
    b, l, _ = p3.shape
    tq = MOBA_QB * MOBA_BLOCK
    assert l % tq == 0
    nblk = l // MOBA_BLOCK
    n_tiles = l // tq
    dh = HEAD_DIM
    kern = functools.partial(_moba_prompt_kernel, nblk=nblk, k_sel=min(MOBA_TOPK, nblk - 1), scale=dh ** -0.5)
    return pl.pallas_call(
        kern,
        grid=(b, n_heads, n_tiles + 1),
        in_specs=[
            pl.BlockSpec((None, tq, dh), lambda bb, h, i: (bb, jnp.minimum(i, n_tiles - 1), h)),
            pl.BlockSpec((None, l, dh), lambda bb, h, i: (bb, 0, n_heads + h)),
            pl.BlockSpec((None, l, dh), lambda bb, h, i: (bb, 0, 2 * n_heads + h)),
        ],
        out_specs=pl.BlockSpec((None, tq, dh), lambda bb, h, i: (bb, jnp.maximum(i - 1, 0), h)),
        out_shape=jax.ShapeDtypeStruct((b, l, n_heads * dh), BF16),
        scratch_shapes=[
            pltpu.VMEM((l, dh), BF16),
            pltpu.VMEM((n_tiles, dh, tq), BF16),
            pltpu.VMEM((nblk, dh), F32),
            pltpu.VMEM((nblk, tq), F32),
            pltpu.VMEM((tq, tq), F32),
            pltpu.VMEM((n_tiles, tq, tq), F32),
            pltpu.VMEM((n_tiles, tq, tq), F32),
            pltpu.VMEM((2, 1, tq), F32),
            pltpu.VMEM((dh, tq), F32),
        ],
        compiler_params=_params("parallel", "parallel", "arbitrary"),
        name="moba_prompt",
    )(p3, p3, p3)


def _moba_sample_kernel(pt_ref, q_ref, kn_ref, vn_ref, *rest, n_steps, n_heads, n_new, page, k_sel, scale):
    g = PAGES_PER_STEP
    k_refs = rest[:g]
    v_refs = rest[g:2 * g]
    o_ref = rest[2 * g]
    s_all, ksum, selb, headb, newpad, acc, lacc, m_scr, lnew = rest[2 * g + 1:]
    ph = pl.program_id(1)
    st = pl.program_id(2)
    dh = HEAD_DIM
    rows = n_new * n_heads
    lanes = page * n_heads
    ppb = MOBA_BLOCK // page
    n_pages = n_steps * g
    n_past = n_pages // ppb
    vw = selb.shape[2]

    def same_head(shape):
        r = lax.broadcasted_iota(jnp.int32, shape, 0)
        c = lax.broadcasted_iota(jnp.int32, shape, 1)
        return (r // n_new) == (c % n_heads)

    def qscaled():
        return (q_ref[...] * (scale * LOG2E)).astype(BF16)

    def page_bias(pidx):
        return jnp.concatenate([selb[pidx // ppb]] * (lanes // vw), axis=1)

    @pl.when(ph == 0)
    def _():
        qb16 = qscaled()
        for u in range(g):
            kp = k_refs[u][...]
            s_all[st * g + u] = lax.dot_general(qb16, kp.astype(BF16), _NT, preferred_element_type=F32)
            srow = jnp.sum(kp.reshape(page, n_heads, dh), axis=0)
            dst = pl.ds(pl.multiple_of((st * (g // ppb) + u // ppb) * n_heads, n_heads), n_heads)
            if u % ppb == 0:
                ksum[dst, :] = srow
            else:
                ksum[dst, :] = ksum[dst, :] + srow

    @pl.when((ph == 1) & (st == 0))
    def _():
        kmean = ksum[...] * (1.0 / MOBA_BLOCK)
        sc2 = lax.dot_general(q_ref[...], kmean, _NT, precision=lax.Precision.HIGHEST,
                              preferred_element_type=F32)
        sc2 = jnp.where(same_head(sc2.shape), sc2, 0.0)
        gr = lax.broadcasted_iota(jnp.int32, (n_past * n_heads, n_past), 0)
        gc = lax.broadcasted_iota(jnp.int32, (n_past * n_heads, n_past), 1)
        pick = jnp.where(gr // n_heads == gc, 1.0, 0.0)
        sc = jnp.dot(sc2, pick, precision=lax.Precision.HIGHEST, preferred_element_type=F32)
        jj = lax.broadcasted_iota(jnp.int32, sc.shape, 1)
        rank = jnp.zeros(sc.shape, jnp.int32)
        for j2 in range(n_past):
            sj = sc[:, j2:j2 + 1]
            beats = (sj > sc) | ((sj == sc) & (j2 < jj))
            rank = rank + beats.astype(jnp.int32)
        sel = jnp.where(rank < k_sel, 0.0, MASK_FILL)
        for j2 in range(n_past):
            selb[j2] = jnp.broadcast_to(sel[:, j2:j2 + 1], (rows, vw))
        headb[...] = jnp.where(same_head((rows, lanes)), 0.0, MASK_FILL)

        qb16 = qscaled()
        newpad[...] = jnp.zeros(newpad.shape, F32)
        newpad[0:rows, :] = kn_ref[...]
        s_new = lax.dot_general(qb16, newpad[...].astype(BF16), _NT, preferred_element_type=F32)
        r = lax.broadcasted_iota(jnp.int32, s_new.shape, 0)
        c = lax.broadcasted_iota(jnp.int32, s_new.shape, 1)
        s_new = jnp.where(same_head(s_new.shape) & (c // n_heads <= r % n_new), s_new, MASK_FILL)

        def running_max(pidx, mx):
            return jnp.maximum(mx, s_all[pidx] + page_bias(pidx))

        mx = lax.fori_loop(0, n_pages, running_max, jnp.full((rows, lanes), 2 * MASK_FILL, F32))
        m = jnp.maximum(jnp.max(mx + headb[...], axis=-1, keepdims=True), jnp.max(s_new, axis=-1, keepdims=True))
        m_scr[...] = m
        p_new = jnp.exp2(s_new - m)
        lnew[...] = jnp.sum(p_new, axis=-1, keepdims=True)
        newpad[...] = jnp.zeros(newpad.shape, F32)
        newpad[0:rows, :] = vn_ref[...]
        acc[...] = jnp.dot(p_new.astype(BF16), newpad[...].astype(BF16), preferred_element_type=F32)
        lacc[...] = jnp.zeros(lacc.shape, F32)

    @pl.when(ph == 1)
    def _():
        a = acc[...]
        ls = lacc[...]
        off = headb[...] - m_scr[...]
        for u in range(g):
            pidx = st * g + u
            p = jnp.exp2(s_all[pidx] + page_bias(pidx) + off)
            ls = ls + p
            a = a + jnp.dot(p.astype(BF16), v_refs[u][...].astype(BF16), preferred_element_type=F32)
        acc[...] = a
        lacc[...] = ls

    @pl.when((ph == 1) & (st == n_steps - 1))
    def _():
        o_ref[...] = acc[...] / (jnp.sum(lacc[...], axis=-1, keepdims=True) + lnew[...])


def _moba_sample(p3, cache_k, cache_v, layer, page_table, n_heads):
    b, n_new, _ = p3.shape
    dh = HEAD_DIM
    width = n_heads * dh
    lanes = cache_k.shape[2]
    page = lanes // n_heads
    n_pages = page_table.shape[1]
    g = PAGES_PER_STEP
    ppb = MOBA_BLOCK // page
    rows = n_new * n_heads
    vw = 128
    assert MOBA_BLOCK % page == 0 and g % ppb == 0 and n_pages % g == 0 and rows <= vw and lanes % vw == 0
    n_steps = n_pages // g
    n_past = n_pages // ppb
    kern = functools.partial(_moba_sample_kernel, n_steps=n_steps, n_heads=n_heads, n_new=n_new, page=page,
                             k_sel=min(MOBA_TOPK, n_past), scale=dh ** -0.5)

    heads = lambda c0: p3[:, :, c0:c0 + width].reshape(b, n_new, n_heads, dh)
    q_rows = heads(0).transpose(0, 2, 1, 3).reshape(b, rows, dh)
    kn_rows = heads(width).reshape(b, rows, dh)
    vn_rows = heads(2 * width).reshape(b, rows, dh)

    def k_map(u):
        return lambda bb, ph, st, pt: (layer, pt[bb, jnp.where(ph == 0, st, n_steps - 1) * g + u], 0, 0)

    def v_map(u):
        return lambda bb, ph, st, pt: (layer, pt[bb, jnp.where(ph == 0, 0, st) * g + u], 0, 0)

    new_spec = pl.BlockSpec((None, rows, dh), lambda bb, ph, st, pt: (bb, 0, 0))
    page_spec = lambda imap: pl.BlockSpec((None, None, lanes, dh), imap)
    grid_spec = pltpu.PrefetchScalarGridSpec(
        num_scalar_prefetch=1,
        grid=(b, 2, n_steps),
        in_specs=[new_spec, new_spec, new_spec]
                 + [page_spec(k_map(u)) for u in range(g)] + [page_spec(v_map(u)) for u in range(g)],
        out_specs=new_spec,
        scratch_shapes=[
            pltpu.VMEM((n_pages, rows, lanes), F32),
            pltpu.VMEM((n_past * n_heads, dh), F32),
            pltpu.VMEM((n_past, rows, vw), F32),
            pltpu.VMEM((rows, lanes), F32),
            pltpu.VMEM((vw, dh), F32),
            pltpu.VMEM((rows, dh), F32),
            pltpu.VMEM((rows, lanes), F32),
            pltpu.VMEM((rows, 1), F32),
            pltpu.VMEM((rows, 1), F32),
        ],
    )
    o = pl.pallas_call(
        kern,
        grid_spec=grid_spec,
        out_shape=jax.ShapeDtypeStruct((b, rows, dh), F32),
        compiler_params=_params("parallel", "arbitrary", "arbitrary"),
        name="moba_sample",
    )(page_table, q_rows, kn_rows, vn_rows, *([cache_k] * g), *([cache_v] * g))
    return o.reshape(b, n_heads, n_new, dh).transpose(0, 2, 1, 3).reshape(b, n_new, width)


def _hgrn_levels():
    sizes, h = [], HG_SUB
    while h < HG_CHUNK:
        sizes.append(h)
        h *= 2
    return sizes


def _hgrn_prompt_kernel(xq_ref, xf_ref, xi_ref, xg_ref, lb_ref, gn_ref, s0_ref, o_ref, s_ref,
                        st_scr, st_old, tril_scr, group_scr, diag_scr):
    t = pl.program_id(2)
    n = xq_ref.shape[0]
    dh = HEAD_DIM
    c = HG_CHUNK
    levels = _hgrn_levels()

    @pl.when(t == 0)
    def _():
        for hh in range(HG_HEADS_PER_STEP):
            st_scr[hh] = s0_ref[hh].T
        r = lax.broadcasted_iota(jnp.int32, (n, n), 0)
        cc = lax.broadcasted_iota(jnp.int32, (n, n), 1)
        tril_scr[...] = jnp.where((cc <= r) & (r // c == cc // c), 1.0, 0.0).astype(BF16)
        diag_scr[...] = jnp.where((cc <= r) & (r // HG_SUB == cc // HG_SUB), 1.0, 0.0)
        for li, hs in enumerate(levels):
            group_scr[li] = jnp.where(r // (2 * hs) == cc // (2 * hs), 1.0, 0.0)

    heads = range(HG_HEADS_PER_STEP)
    cols = [slice(hh * dh, (hh + 1) * dh) for hh in heads]

    def finish(hh, o):
        y = o * lax.rsqrt(jnp.mean(o * o, axis=-1, keepdims=True) + EPS) * gn_ref[...]
        o_ref[:, cols[hh]] = (y * _sigmoid(xg_ref[:, cols[hh]])).astype(o_ref.dtype)

    qkg = [_hgrn_gates(xq_ref[:, cs], xf_ref[:, cs], lb_ref[:, cs]) for cs in cols]
    vb = [xi_ref[:, cs].astype(BF16) for cs in cols]
    b = [_hgrn_cumsum(g2, tril_scr) for _, _, g2 in qkg]
    grow = [_hgrn_tile_grow(bh) for bh in b]
    a = [_hgrn_level_weights(q, k, bh, group_scr) for (q, k, _), bh in zip(qkg, b)]
    ad = [_hgrn_tile_weights(q, k, jnp.minimum(gh, HG_MAX_GROW), diag_scr) for (q, k, _), gh in zip(qkg, grow)]
    o = [jnp.dot((ah + adh).astype(BF16), vh, preferred_element_type=F32) for ah, adh, vh in zip(a, ad, vb)]
    for hh in heads:
        st_old[hh] = st_scr[hh]
        oh, st = _hgrn_carry(qkg[hh][0], qkg[hh][1], b[hh], vb[hh], o[hh], st_scr[hh])
        st_scr[hh] = st
        finish(hh, oh)

    for hh in heads:
        @pl.when(jnp.max(grow[hh]) > HG_MAX_GROW)
        def _(hh=hh):
            cs = cols[hh]
            q, k, g2 = _hgrn_gates(xq_ref[:, cs], xf_ref[:, cs], lb_ref[:, cs])
            v = xi_ref[:, cs]
            bh = _hgrn_cumsum(g2, tril_scr)
            o_lv = jnp.dot(_hgrn_level_weights(q, k, bh, group_scr).astype(BF16), v.astype(BF16),
                           preferred_element_type=F32)
            oh, _ = _hgrn_carry(q, k, bh, v.astype(BF16), o_lv + _hgrn_tile_pairs(q, k, bh, v), st_old[hh])
            finish(hh, oh)

    @pl.when(t == pl.num_programs(2) - 1)
    def _():
        for hh in heads:
            s_ref[hh] = st_scr[hh].T


def _hgrn_gates(xq, xf, lb):
    f = lb + (1.0 - lb) * _sigmoid(xf)
    return xq * _sigmoid(xq), 1.0 - f, jnp.log2(f)


def _hgrn_cumsum(g2, tril_scr):
    b = None
    resid = g2
    for _ in range(3):
        part = resid.astype(BF16)
        resid = resid - part.astype(F32)
        term = jnp.dot(tril_scr[...], part, preferred_element_type=F32)
        b = term if b is None else b + term
    return b


def _hgrn_level_weights(q, k, b, group_scr):
    n, dh = q.shape
    a = None
    for li, hs in enumerate(_hgrn_levels()):
        zero = jnp.zeros((hs, dh), F32)
        q_rows, k_rows = [], []
        for g0 in range(0, n, 2 * hs):
            first, second = slice(g0, g0 + hs), slice(g0 + hs, g0 + 2 * hs)
            bref = b[g0 + hs - 1:g0 + hs, :]
            q_rows += [zero, q[second] * jnp.exp2(b[second] - bref)]
            k_rows += [k[first] * jnp.exp2(bref - b[first]), zero]
        ql = jnp.concatenate(q_rows, axis=0).astype(BF16)
        kl = jnp.concatenate(k_rows, axis=0).astype(BF16)
        al = lax.dot_general(ql, kl, _NT, preferred_element_type=F32) * group_scr[li]
        a = al if a is None else a + al
    return a


def _hgrn_tile_grow(b):
    n, dh = b.shape
    zero_row = jnp.zeros((1, dh), F32)
    b0 = jnp.concatenate([jnp.broadcast_to(zero_row if r0 % HG_CHUNK == 0 else b[r0 - 1:r0, :], (HG_SUB, dh))
                          for r0 in range(0, n, HG_SUB)], axis=0)
    return b0 - b


def _hgrn_tile_weights(q, k, grow, diag_scr):
    qd = (q * jnp.exp2(-grow)).astype(BF16)
    kd = (k * jnp.exp2(grow)).astype(BF16)
    return lax.dot_general(qd, kd, _NT, preferred_element_type=F32) * diag_scr[...]


def _hgrn_tile_pairs(q, k, b, v):
    n, dh = q.shape
    sub = HG_SUB
    q3, k3, b3, v3 = (x.reshape(n // sub, sub, dh) for x in (q, k, b, v))
    tio = lax.broadcasted_iota(jnp.int32, q3.shape, 1)
    od = jnp.zeros(q3.shape, F32)
    for s in range(sub):
        e = jnp.exp2(jnp.where(tio >= s, b3 - b3[:, s:s + 1, :], MASK_FILL))
        w = q3 * k3[:, s:s + 1, :] * e
        od = od + jnp.sum(w, axis=-1, keepdims=True) * v3[:, s:s + 1, :]
    return od.reshape(n, dh)


def _hgrn_carry(q, k, b, vb, o, st):
    n = q.shape[0]
    c = HG_CHUNK
    parts = []
    for c0 in range(0, n, c):
        rows = slice(c0, c0 + c)
        bc = b[rows]
        b_end = bc[c - 1:c, :]
        qd = (q[rows] * jnp.exp2(bc)).astype(BF16)
        parts.append(o[rows] + lax.dot_general(qd, st.astype(BF16), _NT, preferred_element_type=F32))
        kd = (k[rows] * jnp.exp2(b_end - bc)).astype(BF16)
        st = st * jnp.exp2(b_end) + lax.dot_general(vb[rows], kd, _TN, preferred_element_type=F32)
    return jnp.concatenate(parts, axis=0), st


def _hgrn_prompt(p3, lb, gn, s0, n_heads, col0):
    b, l, _ = p3.shape
    dh = HEAD_DIM
    step = min(HG_STEP, l)
    assert l % step == 0 and step % HG_CHUNK == 0
    hps = HG_HEADS_PER_STEP
    assert n_heads % hps == 0 and col0 % (hps * dh) == 0
    c0 = col0 // (hps * dh)
    spec = lambda off: pl.BlockSpec((None, step, hps * dh), lambda bb, h, t: (bb, t, c0 + off * (n_heads // hps) + h))
    vec = pl.BlockSpec((1, hps * dh), lambda bb, h, t: (0, h))
    state = pl.BlockSpec((None, hps, dh, dh), lambda bb, h, t: (bb, h, 0, 0))
    return pl.pallas_call(
        _hgrn_prompt_kernel,
        grid=(b, n_heads // hps, l // step),
        in_specs=[spec(0), spec(1), spec(2), spec(3), vec, pl.BlockSpec((1, dh), lambda bb, h, t: (0, 0)), state],
        out_specs=[pl.BlockSpec((None, step, hps * dh), lambda bb, h, t: (bb, t, h)), state],
        out_shape=[jax.ShapeDtypeStruct((b, l, n_heads * dh), BF16), jax.ShapeDtypeStruct(s0.shape, F32)],
        scratch_shapes=[pltpu.VMEM((hps, dh, dh), F32),
                        pltpu.VMEM((hps, dh, dh), F32),
                        pltpu.VMEM((step, step), BF16),
                        pltpu.VMEM((len(_hgrn_levels()), step, step), F32),
                        pltpu.VMEM((step, step), F32)],
        compiler_params=_params("parallel", "parallel", "arbitrary"),
        name="hgrn_prompt",
    )(p3, p3, p3, p3, lb.reshape(1, -1), gn.reshape(1, dh), s0)


def _hgrn_sample_kernel(xq_ref, xf_ref, xi_ref, xg_ref, lb_ref, gn_ref, s0_ref, o_ref, s_ref):
    n_new = xq_ref.shape[0]
    dh = HEAD_DIM
    pad = jnp.zeros((8 - n_new % 8, dh), F32)
    cols = lambda a: jnp.concatenate([a, pad], axis=0).T
    for h in range(s0_ref.shape[0]):
        cs = slice(h * dh, (h + 1) * dh)
        xq = xq_ref[:, cs]
        f = lb_ref[:, cs] + (1.0 - lb_ref[:, cs]) * _sigmoid(xf_ref[:, cs])
        q = xq * _sigmoid(xq)
        f_c, k_c, q_c = cols(f), cols(1.0 - f), cols(q)
        v = xi_ref[:, cs]
        s = s0_ref[h]
        outs = []
        for t in range(n_new):
            s = f_c[:, t:t + 1] * s + k_c[:, t:t + 1] * v[t:t + 1, :]
            outs.append(jnp.sum(q_c[:, t:t + 1] * s, axis=0, keepdims=True))
        o = jnp.concatenate(outs, axis=0)
        y = o * lax.rsqrt(jnp.mean(o * o, axis=-1, keepdims=True) + EPS) * gn_ref[...]
        o_ref[:, cs] = y * _sigmoid(xg_ref[:, cs])
        s_ref[h] = s


def _hgrn_sample(p3, lb, gn, state_all, layer, n_heads, col0):
    b, n_new, _ = p3.shape
    dh = HEAD_DIM
    width = n_heads * dh
    assert col0 % width == 0
    spec = lambda off: pl.BlockSpec((None, n_new, width), lambda bb: (bb, 0, col0 // width + off))
    return pl.pallas_call(
        _hgrn_sample_kernel,
        grid=(b,),
        in_specs=[spec(0), spec(1), spec(2), spec(3),
                  pl.BlockSpec((1, width), lambda bb: (0, 0)), pl.BlockSpec((1, dh), lambda bb: (0, 0)),
                  pl.BlockSpec((None, None, n_heads, dh, dh), lambda bb: (layer, bb, 0, 0, 0))],
        out_specs=[pl.BlockSpec((None, n_new, width), lambda bb: (bb, 0, 0)),
                   pl.BlockSpec((None, n_heads, dh, dh), lambda bb: (bb, 0, 0, 0))],
        out_shape=[jax.ShapeDtypeStruct((b, n_new, width), F32),
                   jax.ShapeDtypeStruct(state_all.shape[1:], F32)],
        compiler_params=_params("parallel"),
        name="hgrn_sample",
    )(p3, p3, p3, p3, lb.reshape(1, -1), gn.reshape(1, dh), state_all)


def _mem_attn_kernel(q_ref, k_ref, v_ref, g_ref, o_ref, *, scale):
    dm = g_ref.shape[1]
    for h in range(q_ref.shape[1] // dm):
        cols = slice(h * dm, (h + 1) * dm)
        q = q_ref[:, cols]
        qn = q * lax.rsqrt(jnp.mean(q * q, axis=-1, keepdims=True) + EPS) * g_ref[...]
        s = lax.dot_general(qn.astype(BF16), k_ref[:, cols].astype(BF16), _NT, preferred_element_type=F32) * scale
        p = jnp.exp(s - jnp.max(s, axis=-1, keepdims=True))
        o = jnp.dot(p.astype(BF16), v_ref[:, cols].astype(BF16), preferred_element_type=F32)
        o_ref[:, cols] = (o / jnp.sum(p, axis=-1, keepdims=True)).astype(o_ref.dtype)


def _mem_attn(p3, mem_k, mem_v, layer, gn, col0, tq, out_dtype):
    b, l, _ = p3.shape
    mtok, width = mem_k.shape[2], mem_k.shape[3]
    dm = width // MEM_HEADS
    kern = functools.partial(_mem_attn_kernel, scale=dm ** -0.5)
    kv = pl.BlockSpec((None, None, mtok, width), lambda bb, i: (layer, bb, 0, 0))
    return pl.pallas_call(
        kern,
        grid=(b, l // tq),
        in_specs=[pl.BlockSpec((None, tq, width), lambda bb, i: (bb, i, col0 // width)), kv, kv,
                  pl.BlockSpec((1, dm), lambda bb, i: (0, 0))],
        out_specs=pl.BlockSpec((None, tq, width), lambda bb, i: (bb, i, 0)),
        out_shape=jax.ShapeDtypeStruct((b, l, width), out_dtype),
        compiler_params=_params("parallel", "arbitrary"),
        name="mem_attn",
    )(p3, mem_k, mem_v, gn.reshape(1, dm))


def _merge_kernel(oa_ref, oh_ref, om_ref, ga_ref, gh_ref, gm_ref, wa_ref, wh_ref, wm_ref, o_ref):
    acc = _sigmoid(ga_ref[...]) * jnp.dot(oa_ref[...].astype(BF16), wa_ref[...], preferred_element_type=F32)
    acc = acc + _sigmoid(gh_ref[...]) * jnp.dot(oh_ref[...].astype(BF16), wh_ref[...], preferred_element_type=F32)
    acc = acc + _sigmoid(gm_ref[...]) * jnp.dot(om_ref[...].astype(BF16), wm_ref[...], preferred_element_type=F32)
    o_ref[...] = acc.astype(o_ref.dtype)


def _merge(o_a, o_h, o_m, p2, gate_col0, w_branch, layer, tm, tn):
    n, bw = o_a.shape
    d = w_branch.shape[3]
    gspec = lambda br: pl.BlockSpec((tm, tn), lambda i, j: (i, (gate_col0 + br * d) // tn + j))
    wspec = lambda br: pl.BlockSpec((None, None, bw, tn), lambda i, j: (layer, br, 0, j))
    ospec = pl.BlockSpec((tm, bw), lambda i, j: (i, 0))
    return pl.pallas_call(
        _merge_kernel,
        grid=(n // tm, d // tn),
        in_specs=[ospec, ospec, ospec, gspec(0), gspec(1), gspec(2), wspec(0), wspec(1), wspec(2)],
        out_specs=pl.BlockSpec((tm, tn), lambda i, j: (i, j)),
        out_shape=jax.ShapeDtypeStruct((n, d), BF16),
        compiler_params=_params("parallel", "arbitrary"),
        name="merge",
    )(o_a, o_h, o_m, p2, p2, p2, w_branch, w_branch, w_branch)


def _out_proj_kernel(m_ref, w_ref, x_ref, g_ref, y_ref, h_ref):
    y = x_ref[...] + jnp.dot(m_ref[...], w_ref[...], preferred_element_type=F32)
    y_ref[...] = y
    hn = y * lax.rsqrt(jnp.mean(y * y, axis=-1, keepdims=True) + EPS)
    h_ref[...] = (hn * g_ref[...]).astype(h_ref.dtype)


def _out_proj(m, w_out, layer, x, g, tm):
    n, d = x.shape
    row = pl.BlockSpec((tm, d), lambda i: (i, 0))
    return pl.pallas_call(
        _out_proj_kernel,
        grid=(n // tm,),
        in_specs=[row, pl.BlockSpec((None, d, d), lambda i: (layer, 0, 0)), row, pl.BlockSpec((1, d), lambda i: (0, 0))],
        out_specs=[row, row],
        out_shape=[jax.ShapeDtypeStruct((n, d), F32), jax.ShapeDtypeStruct((n, d), BF16)],
        compiler_params=_params("parallel"),
        name="out_proj",
    )(m, w_out, x, g.reshape(1, d))


def _mlp_kernel(h_ref, wu_ref, wd_ref, x_ref, g_ref, y_ref, hn_ref, acc):
    f = pl.program_id(1)

    @pl.when(f == 0)
    def _():
        acc[...] = jnp.zeros(acc.shape, F32)

    u = jnp.maximum(jnp.dot(h_ref[...], wu_ref[...], preferred_element_type=F32), 0.0)
    acc[...] += jnp.dot((u * u).astype(BF16), wd_ref[...], preferred_element_type=F32)

    @pl.when(f == pl.num_programs(1) - 1)
    def _():
        y = x_ref[...] + acc[...]
        y_ref[...] = y
        hn = y * lax.rsqrt(jnp.mean(y * y, axis=-1, keepdims=True) + EPS)
        hn_ref[...] = (hn * g_ref[...]).astype(hn_ref.dtype)


def _mlp(h, w_up, w_down, layer, x, g_next, tm, tf):
    n, d = x.shape
    dff = w_up.shape[2]
    row = pl.BlockSpec((tm, d), lambda i, f: (i, 0))
    return pl.pallas_call(
        _mlp_kernel,
        grid=(n // tm, dff // tf),
        in_specs=[row, pl.BlockSpec((None, d, tf), lambda i, f: (layer, 0, f)),
                  pl.BlockSpec((None, tf, d), lambda i, f: (layer, f, 0)), row,
                  pl.BlockSpec((1, d), lambda i, f: (0, 0))],
        out_specs=[row, row],
        out_shape=[jax.ShapeDtypeStruct((n, d), F32), jax.ShapeDtypeStruct((n, d), BF16)],
        scratch_shapes=[pltpu.VMEM((tm, d), F32)],
        compiler_params=_params("parallel", "arbitrary"),
        name="mlp",
    )(h, w_up, w_down, x, g_next.reshape(1, d))


def _tiles(n):
    return (min(n, 1024), min(n, 1024), min(n, 256), min(n, 512))


def _trunk_layer(layer, x, h, shape, w, lb, next_gain, attn_fn, hgrn_fn, mem_fn):
    b, l = shape
    n, d = x.shape
    t_proj, t_merge, t_out, t_mlp = _tiles(n)
    n_heads = w["hg_width"] // HEAD_DIM
    width = n_heads * HEAD_DIM
    qk_gain = jnp.concatenate([jnp.tile(w["q_norm_moba"][layer], n_heads), jnp.tile(w["k_norm_moba"][layer], n_heads)])
    if n >= width:
        p2, (k_new, v_new) = _proj(h, w["w_in"], layer, qk_gain.reshape(1, -1), HEAD_DIM, t_proj, width, n_copies=2)
    else:
        p2, _ = _proj(h, w["w_in"], layer, qk_gain.reshape(1, -1), HEAD_DIM, t_proj, 2 * width)
        k_new, v_new = p2[:, width:2 * width], p2[:, 2 * width:3 * width]
    kv = lambda a: a.reshape(b, l, n_heads, HEAD_DIM)
    p3 = p2.reshape(b, l, -1)
    o_a = attn_fn(p3)
    o_h, s_new = hgrn_fn(p3, lb)
    o_m = mem_fn(p3)
    flat = lambda a: a.reshape(n, -1)
    m = _merge(flat(o_a), flat(o_h), flat(o_m), p2, w["gate_col0"], w["w_branch"], layer, t_merge, 512)
    x, h2 = _out_proj(m, w["w_out"], layer, x, w["ln_mlp"][layer], t_out)
    x, h_next = _mlp(h2, w["w_up"], w["w_down"], layer, x, next_gain, t_mlp, 1024)
    return x, h_next, kv(k_new), kv(v_new), s_new


def kernel(x_prompt, x_sample, mem_prompt, cache_moba_k, cache_moba_v, cache_mem_k, cache_mem_v, state_hgrn,
           page_table, ln_mix, w_in, q_norm_moba, k_norm_moba, lb_logits, hg_out_norm, q_norm_mem, ln_mem,
           w_mem_kv, k_norm_mem, w_branch, w_out, ln_mlp, w_up, w_down):
    depth = w_in.shape[0]
    bp, lp, d = x_prompt.shape
    bs, ls, _ = x_sample.shape
    hg_width = lb_logits.shape[1]
    n_heads = hg_width // HEAD_DIM
    moba_width = n_heads * HEAD_DIM
    mem_width = cache_mem_k.shape[3] * cache_mem_k.shape[4]
    mem_tokens = mem_prompt.shape[1]
    hg_col0 = 3 * moba_width
    mem_col0 = hg_col0 + 4 * hg_width
    gate_col0 = mem_col0 + mem_width

    w = dict(
        w_in=w_in.astype(BF16), w_branch=w_branch.astype(BF16), w_out=w_out.astype(BF16),
        w_up=w_up.astype(BF16), w_down=w_down.astype(BF16), ln_mlp=ln_mlp,
        q_norm_moba=q_norm_moba, k_norm_moba=k_norm_moba, hg_width=hg_width, gate_col0=gate_col0,
    )
    w_mem = w_mem_kv.astype(BF16)
    lower = _lower_bounds(lb_logits)

    ck = cache_moba_k.reshape(cache_moba_k.shape[:2] + (-1, HEAD_DIM))
    cv = cache_moba_v.reshape(cache_moba_v.shape[:2] + (-1, HEAD_DIM))
    cmk = cache_mem_k.reshape(cache_mem_k.shape[:3] + (mem_width,))
    cmv = cache_mem_v.reshape(cache_mem_v.shape[:3] + (mem_width,))

    xp = x_prompt.reshape(bp * lp, d)
    xs = x_sample.reshape(bs * ls, d)
    hp = _rmsnorm_bf16(xp, ln_mix[0], min(bp * lp, 512))
    hs = _rmsnorm_bf16(xs, ln_mix[0], bs * ls)
    s0_p = jnp.zeros((bp,) + state_hgrn.shape[2:], F32)
    mem_flat = mem_prompt.reshape(bp * mem_tokens, d)

    kp_l, vp_l, mkp_l, mvp_l, sp_l, ks_l, vs_l, ss_l = [], [], [], [], [], [], [], []
    for layer in range(depth):
        next_gain = ln_mix[layer + 1] if layer + 1 < depth else ln_mix[layer]
        gn_h = hg_out_norm[layer]
        gn_m = q_norm_mem[layer]

        hm = _rmsnorm_bf16(mem_flat, ln_mem[layer], min(bp * mem_tokens, 512))
        mem_gain = jnp.tile(k_norm_mem[layer], MEM_HEADS).reshape(1, -1)
        mkv, _ = _proj(hm, w_mem, layer, mem_gain, mem_width // MEM_HEADS, min(bp * mem_tokens, 512), mem_width)
        mk = mkv[:, :mem_width].reshape(1, bp, mem_tokens, mem_width)
        mv = mkv[:, mem_width:].reshape(1, bp, mem_tokens, mem_width)

        xp, hp, kp, vp, sp = _trunk_layer(
            layer, xp, hp, (bp, lp), w, lower[layer], next_gain,
            lambda p3: _moba_prompt(p3, n_heads),
            lambda p3, lb: _hgrn_prompt(p3, lb, gn_h, s0_p, n_heads, hg_col0),
            lambda p3: _mem_attn(p3, mk, mv, 0, gn_m, mem_col0, min(lp, 512), BF16))
        xs, hs, ks, vs, ss = _trunk_layer(
            layer, xs, hs, (bs, ls), w, lower[layer], next_gain,
            lambda p3: _moba_sample(p3, ck, cv, layer, page_table, n_heads),
            lambda p3, lb: _hgrn_sample(p3, lb, gn_h, state_hgrn, layer, n_heads, hg_col0),
            lambda p3: _mem_attn(p3, cmk, cmv, layer, gn_m, mem_col0, ls, F32))

        kp_l.append(kp)
        vp_l.append(vp)
        ks_l.append(ks)
        vs_l.append(vs)
        mem_shape = (bp, mem_tokens, MEM_HEADS, mem_width // MEM_HEADS)
        mkp_l.append(mk.reshape(mem_shape))
        mvp_l.append(mv.reshape(mem_shape))
        sp_l.append(sp)
        ss_l.append(ss)

    return (xp.reshape(bp, lp, d), xs.reshape(bs, ls, d), jnp.stack(kp_l), jnp.stack(vp_l), jnp.stack(mkp_l),
            jnp.stack(mvp_l), jnp.stack(sp_l), jnp.stack(ks_l), jnp.stack(vs_l), jnp.stack(ss_l))
```

```python
import functools

import jax
import jax.numpy as jnp
from jax import lax
from jax.experimental import pallas as pl
from jax.experimental.pallas import tpu as pltpu

F32 = jnp.float32
BF16 = jnp.bfloat16

EPS = 1e-6
MASK_FILL = -1e30
MOBA_BLOCK = 256
MOBA_TOPK = 3
HEAD_DIM = 128
MEM_HEADS = 4
MOBA_QB = 2
LOG2E = 1.4426950408889634
HG_CHUNK = 64
HG_SUB = 8
HG_STEP = 256
HG_HEADS_PER_STEP = 4
HG_MAX_GROW = 100.0
PAGES_PER_STEP = 16
VMEM_LIMIT_BYTES = 56 * 1024 * 1024

_NT = (((1,), (1,)), ((), ()))
_TN = (((0,), (0,)), ((), ()))


def _params(*sem):
    return pltpu.CompilerParams(dimension_semantics=sem, vmem_limit_bytes=VMEM_LIMIT_BYTES)


def _sigmoid(x):
    return 1.0 / (1.0 + jnp.exp(-x))


def _lower_bounds_kernel(lg_ref, out_ref):
    x = lg_ref[...]
    e = jnp.exp(x - jnp.max(x, axis=0, keepdims=True))
    p = e / jnp.sum(e, axis=0, keepdims=True)
    c = p[0:1]
    rows = [c - p[0:1]]
    for l in range(1, x.shape[0]):
        c = c + p[l:l + 1]
        rows.append(c - p[0:1])
    out_ref[...] = jnp.concatenate(rows, axis=0)


def _lower_bounds(lb_logits):
    return pl.pallas_call(
        _lower_bounds_kernel,
        out_shape=jax.ShapeDtypeStruct(lb_logits.shape, F32),
        name="lower_bounds",
    )(lb_logits)


def _rmsnorm_kernel(x_ref, g_ref, o_ref):
    x = x_ref[...]
    y = x * lax.rsqrt(jnp.mean(x * x, axis=-1, keepdims=True) + EPS)
    o_ref[...] = (y * g_ref[...]).astype(o_ref.dtype)


def _rmsnorm_bf16(x, g, tm):
    n, d = x.shape
    return pl.pallas_call(
        _rmsnorm_kernel,
        grid=(n // tm,),
        in_specs=[pl.BlockSpec((tm, d), lambda i: (i, 0)), pl.BlockSpec((1, d), lambda i: (0, 0))],
        out_specs=pl.BlockSpec((tm, d), lambda i: (i, 0)),
        out_shape=jax.ShapeDtypeStruct((n, d), BF16),
        compiler_params=_params("parallel"),
        name="rmsnorm",
    )(x, g.reshape(1, d))


def _proj_kernel(h_ref, w_ref, g_ref, o_ref, *copies, n_norm_tiles, group):
    o_ref[...] = jnp.dot(h_ref[...], w_ref[...], preferred_element_type=F32)
    j = pl.program_id(1)

    @pl.when(j < n_norm_tiles)
    def _():
        for c in range(o_ref.shape[1] // group):
            a = o_ref[:, c * group:(c + 1) * group]
            y = a * lax.rsqrt(jnp.mean(a * a, axis=-1, keepdims=True) + EPS)
            o_ref[:, c * group:(c + 1) * group] = y * g_ref[:, c * group:(c + 1) * group]

    n_copies = len(copies) // 2
    for c in range(n_copies):
        @pl.when(j == c + 1)
        def _(c=c):
            copies[c][...] = o_ref[...]
            by_head = copies[n_copies + c]
            for hh in range(by_head.shape[0]):
                by_head[hh] = o_ref[:, hh * group:(hh + 1) * group].astype(by_head.dtype)


def _proj(h, w_all, layer, gains, group, tm, tn, n_copies=0):
    n, k = h.shape
    m = w_all.shape[2]
    n_norm_tiles = gains.shape[1] // tn
    n_groups = tn // group
    kern = functools.partial(_proj_kernel, n_norm_tiles=n_norm_tiles, group=group)
    outs = pl.pallas_call(
        kern,
        grid=(n // tm, m // tn),
        in_specs=[
            pl.BlockSpec((tm, k), lambda i, j: (i, 0)),
            pl.BlockSpec((None, k, tn), lambda i, j: (layer, 0, j)),
            pl.BlockSpec((1, tn), lambda i, j: (0, jnp.minimum(j, n_norm_tiles - 1))),
        ],
        out_specs=[pl.BlockSpec((tm, tn), lambda i, j: (i, j))]
                  + [pl.BlockSpec((tm, tn), lambda i, j: (i, 0))] * n_copies
                  + [pl.BlockSpec((n_groups, tm, group), lambda i, j: (0, i, 0))] * n_copies,
        out_shape=[jax.ShapeDtypeStruct((n, m), F32)] + [jax.ShapeDtypeStruct((n, tn), F32)] * n_copies
                  + [jax.ShapeDtypeStruct((n_groups, n, group), BF16)] * n_copies,
        compiler_params=_params("parallel", "arbitrary"),
        name="proj",
    )(h, w_all, gains)
    return outs[0], outs[1:1 + n_copies], outs[1 + n_copies:]


def _moba_prompt_kernel(q_ref, k_ref, v_ref, o_ref, kbf, vt, kmean, selbias, biased, s_even, s_odd, m_scr, acc_scr,
                        *, nblk, k_sel, scale):
    t = pl.program_id(2)
    n_tiles = pl.num_programs(2) - 1
    blk = MOBA_BLOCK
    tq = MOBA_QB * blk

    @pl.when(t == 0)
    def _():
        for j in range(nblk):
            kj = k_ref[j * blk:(j + 1) * blk, :]
            kbf[j * blk:(j + 1) * blk, :] = kj
            kmean[j:j + 1, :] = jnp.mean(kj.astype(F32), axis=0, keepdims=True)
        for jp in range(nblk // MOBA_QB):
            vt[jp] = v_ref[jp * tq:(jp + 1) * tq, :].astype(F32).T.astype(BF16)
        kpos = lax.broadcasted_iota(jnp.int32, (tq, tq), 0)
        qpos = lax.broadcasted_iota(jnp.int32, (tq, tq), 1)
        biased[...] = jnp.where((kpos // blk == qpos // blk) & (kpos <= qpos), 0.0, 1.0)

    acc_scr[...] = jnp.zeros(acc_scr.shape, F32)

    @pl.when(t < n_tiles)
    def _():
        sc = lax.dot_general(kmean[...], q_ref[...], _NT, precision=lax.Precision.HIGHEST,
                             preferred_element_type=F32)
        jj = lax.broadcasted_iota(jnp.int32, sc.shape, 0)
        own = MOBA_QB * t + lax.broadcasted_iota(jnp.int32, sc.shape, 1) // blk
        past = jj < own
        sc = jnp.where(past, sc, MASK_FILL)
        rank = jnp.zeros(sc.shape, jnp.int32)
        for j2 in range(nblk):
            sj = sc[j2:j2 + 1, :]
            beats = (sj > sc) | ((sj == sc) & (j2 < jj))
            rank = rank + beats.astype(jnp.int32)
        selbias[...] = jnp.where(past & (rank < k_sel), 0.0, MASK_FILL)

    def run(s_cur, s_prev, cur):
        def finish_group(jp, l):
            p = jnp.exp2(s_prev[jp] - m_scr[1 - cur])
            acc_scr[...] += jnp.dot(vt[jp], p.astype(BF16), preferred_element_type=F32)
            return l + jnp.sum(p, axis=0, keepdims=True)

        def write_out(l):
            @pl.when(t > 0)
            def _():
                o_ref[...] = (acc_scr[...] / l).T.astype(o_ref.dtype)

        @pl.when(t < n_tiles)
        def _():
            qb = (q_ref[...] * (scale * LOG2E)).astype(BF16)

            def logits(jp):
                kj = kbf[pl.ds(pl.multiple_of(jp * tq, tq), tq), :]
                return lax.dot_general(kj, qb, _NT, preferred_element_type=F32)

            def bias(jp):
                rows = [jnp.broadcast_to(selbias[pl.ds(jp * MOBA_QB + u, 1), :], (blk, tq))
                        for u in range(MOBA_QB)]
                return jnp.concatenate(rows, axis=0)

            def both(jp, carry):
                m, l = carry
                s = logits(jp) + bias(jp)
                s_cur[jp] = s
                return jnp.maximum(m, jnp.max(s, axis=0, keepdims=True)), finish_group(jp, l)

            m, l = lax.fori_loop(0, t, both, (jnp.full((1, tq), MASK_FILL, F32), jnp.zeros((1, tq), F32)))
            write_out(l)

            s = logits(t) + bias(t) * biased[...]
            s_cur[t] = s
            m_scr[cur] = jnp.maximum(m, jnp.max(s, axis=0, keepdims=True))

        @pl.when(t == n_tiles)
        def _():
            write_out(lax.fori_loop(0, t, finish_group, jnp.zeros((1, tq), F32)))

    @pl.when(t % 2 == 0)
    def _():
        run(s_even, s_odd, 0)

    @pl.when(t % 2 == 1)
    def _():
        run(s_odd, s_even, 1)


def _moba_prompt(p3, k_heads, v_heads, n_heads):
    b, l, _ = p3.shape
    tq = MOBA_QB * MOBA_BLOCK
    assert l % tq == 0
    nblk = l // MOBA_BLOCK
    n_tiles = l // tq
    dh = HEAD_DIM
    kern = functools.partial(_moba_prompt_kernel, nblk=nblk, k_sel=min(MOBA_TOPK, nblk - 1), scale=dh ** -0.5)
    return pl.pallas_call(
        kern,
        grid=(b, n_heads, n_tiles + 1),
        in_specs=[
            pl.BlockSpec((None, tq, dh), lambda bb, h, i: (bb, jnp.minimum(i, n_tiles - 1), h)),
            pl.BlockSpec((None, l, dh), lambda bb, h, i: (h, bb, 0)),
            pl.BlockSpec((None, l, dh), lambda bb, h, i: (h, bb, 0)),
        ],
        out_specs=pl.BlockSpec((None, tq, dh), lambda bb, h, i: (bb, jnp.maximum(i - 1, 0), h)),
        out_shape=jax.ShapeDtypeStruct((b, l, n_heads * dh), BF16),
        scratch_shapes=[
            pltpu.VMEM((l, dh), BF16),
            pltpu.VMEM((n_tiles, dh, tq), BF16),
            pltpu.VMEM((nblk, dh), F32),
            pltpu.VMEM((nblk, tq), F32),
            pltpu.VMEM((tq, tq), F32),
            pltpu.VMEM((n_tiles, tq, tq), F32),
            pltpu.VMEM((n_tiles, tq, tq), F32),
            pltpu.VMEM((2, 1, tq), F32),
            pltpu.VMEM((dh, tq), F32),
        ],
        compiler_params=_params("parallel", "parallel", "arbitrary"),
        name="moba_prompt",
    )(p3, k_heads, v_heads)


def _moba_sample_kernel(pt_ref, q_ref, kn_ref, vn_ref, *rest, n_steps, n_heads, n_new, page, k_sel, scale):
    g = PAGES_PER_STEP
    k_refs = rest[:g]
    v_refs = rest[g:2 * g]
    o_ref = rest[2 * g]
    s_all, ksum, selb, headb, newpad, acc, lacc, m_scr, lnew = rest[2 * g + 1:]
    ph = pl.program_id(1)
    st = pl.program_id(2)
    dh = HEAD_DIM
    rows = n_new * n_heads
    lanes = page * n_heads
    ppb = MOBA_BLOCK // page
    n_pages = n_steps * g
    n_past = n_pages // ppb
    vw = selb.shape[2]

    def same_head(shape):
        r = lax.broadcasted_iota(jnp.int32, shape, 0)
        c = lax.broadcasted_iota(jnp.int32, shape, 1)
        return (r // n_new) == (c % n_heads)

    def qscaled():
        return (q_ref[...] * (scale * LOG2E)).astype(BF16)

    def page_bias(pidx):
        return jnp.concatenate([selb[pidx // ppb]] * (lanes // vw), axis=1)

    @pl.when(ph == 0)
    def _():
        qb16 = qscaled()
        for u in range(g):
            kp = k_refs[u][...]
            s_all[st * g + u] = lax.dot_general(qb16, kp.astype(BF16), _NT, preferred_element_type=F32)
            srow = jnp.sum(kp.reshape(page, n_heads, dh), axis=0)
            dst = pl.ds(pl.multiple_of((st * (g // ppb) + u // ppb) * n_heads, n_heads), n_heads)
            if u % ppb == 0:
                ksum[dst, :] = srow
            else:
                ksum[dst, :] = ksum[dst, :] + srow

    @pl.when((ph == 1) & (st == 0))
    def _():
        kmean = ksum[...] * (1.0 / MOBA_BLOCK)
        sc2 = lax.dot_general(q_ref[...], kmean, _NT, precision=lax.Precision.HIGHEST,
                              preferred_element_type=F32)
        sc2 = jnp.where(same_head(sc2.shape), sc2, 0.0)
        gr = lax.broadcasted_iota(jnp.int32, (n_past * n_heads, n_past), 0)
        gc = lax.broadcasted_iota(jnp.int32, (n_past * n_heads, n_past), 1)
        pick = jnp.where(gr // n_heads == gc, 1.0, 0.0)
        sc = jnp.dot(sc2, pick, precision=lax.Precision.HIGHEST, preferred_element_type=F32)
        jj = lax.broadcasted_iota(jnp.int32, sc.shape, 1)
        rank = jnp.zeros(sc.shape, jnp.int32)
        for j2 in range(n_past):
            sj = sc[:, j2:j2 + 1]
            beats = (sj > sc) | ((sj == sc) & (j2 < jj))
            rank = rank + beats.astype(jnp.int32)
        sel = jnp.where(rank < k_sel, 0.0, MASK_FILL)
        for j2 in range(n_past):
            selb[j2] = jnp.broadcast_to(sel[:, j2:j2 + 1], (rows, vw))
        headb[...] = jnp.where(same_head((rows, lanes)), 0.0, MASK_FILL)

        qb16 = qscaled()
        newpad[...] = jnp.zeros(newpad.shape, F32)
        newpad[0:rows, :] = kn_ref[...]
        s_new = lax.dot_general(qb16, newpad[...].astype(BF16), _NT, preferred_element_type=F32)
        r = lax.broadcasted_iota(jnp.int32, s_new.shape, 0)
        c = lax.broadcasted_iota(jnp.int32, s_new.shape, 1)
        s_new = jnp.where(same_head(s_new.shape) & (c // n_heads <= r % n_new), s_new, MASK_FILL)

        def running_max(pidx, mx):
            return jnp.maximum(mx, s_all[pidx] + page_bias(pidx))

        mx = lax.fori_loop(0, n_pages, running_max, jnp.full((rows, lanes), 2 * MASK_FILL, F32))
        m = jnp.maximum(jnp.max(mx + headb[...], axis=-1, keepdims=True), jnp.max(s_new, axis=-1, keepdims=True))
        m_scr[...] = m
        p_new = jnp.exp2(s_new - m)
        lnew[...] = jnp.sum(p_new, axis=-1, keepdims=True)
        newpad[...] = jnp.zeros(newpad.shape, F32)
        newpad[0:rows, :] = vn_ref[...]
        acc[...] = jnp.dot(p_new.astype(BF16), newpad[...].astype(BF16), preferred_element_type=F32)
        lacc[...] = jnp.zeros(lacc.shape, F32)

    @pl.when(ph == 1)
    def _():
        a = acc[...]
        ls = lacc[...]
        off = headb[...] - m_scr[...]
        for u in range(g):
            pidx = st * g + u
            p = jnp.exp2(s_all[pidx] + page_bias(pidx) + off)
            ls = ls + p
            a = a + jnp.dot(p.astype(BF16), v_refs[u][...].astype(BF16), preferred_element_type=F32)
        acc[...] = a
        lacc[...] = ls

    @pl.when((ph == 1) & (st == n_steps - 1))
    def _():
        o_ref[...] = acc[...] / (jnp.sum(lacc[...], axis=-1, keepdims=True) + lnew[...])


def _moba_sample(p3, cache_k, cache_v, layer, page_table, n_heads):
    b, n_new, _ = p3.shape
    dh = HEAD_DIM
    width = n_heads * dh
    lanes = cache_k.shape[2]
    page = lanes // n_heads
    n_pages = page_table.shape[1]
    g = PAGES_PER_STEP
    ppb = MOBA_BLOCK // page
    rows = n_new * n_heads
    vw = 128
    assert MOBA_BLOCK % page == 0 and g % ppb == 0 and n_pages % g == 0 and rows <= vw and lanes % vw == 0
    n_steps = n_pages // g
    n_past = n_pages // ppb
    kern = functools.partial(_moba_sample_kernel, n_steps=n_steps, n_heads=n_heads, n_new=n_new, page=page,
                             k_sel=min(MOBA_TOPK, n_past), scale=dh ** -0.5)

    heads = lambda c0: p3[:, :, c0:c0 + width].reshape(b, n_new, n_heads, dh)
    q_rows = heads(0).transpose(0, 2, 1, 3).reshape(b, rows, dh)
    kn_rows = heads(width).reshape(b, rows, dh)
    vn_rows = heads(2 * width).reshape(b, rows, dh)

    def k_map(u):
        return lambda bb, ph, st, pt: (layer, pt[bb, jnp.where(ph == 0, st, n_steps - 1) * g + u], 0, 0)

    def v_map(u):
        return lambda bb, ph, st, pt: (layer, pt[bb, jnp.where(ph == 0, 0, st) * g + u], 0, 0)

    new_spec = pl.BlockSpec((None, rows, dh), lambda bb, ph, st, pt: (bb, 0, 0))
    page_spec = lambda imap: pl.BlockSpec((None, None, lanes, dh), imap)
    grid_spec = pltpu.PrefetchScalarGridSpec(
        num_scalar_prefetch=1,
        grid=(b, 2, n_steps),
        in_specs=[new_spec, new_spec, new_spec]
                 + [page_spec(k_map(u)) for u in range(g)] + [page_spec(v_map(u)) for u in range(g)],
        out_specs=new_spec,
        scratch_shapes=[
            pltpu.VMEM((n_pages, rows, lanes), F32),
            pltpu.VMEM((n_past * n_heads, dh), F32),
            pltpu.VMEM((n_past, rows, vw), F32),
            pltpu.VMEM((rows, lanes), F32),
            pltpu.VMEM((vw, dh), F32),
            pltpu.VMEM((rows, dh), F32),
            pltpu.VMEM((rows, lanes), F32),
            pltpu.VMEM((rows, 1), F32),
            pltpu.VMEM((rows, 1), F32),
        ],
    )
    o = pl.pallas_call(
        kern,
        grid_spec=grid_spec,
        out_shape=jax.ShapeDtypeStruct((b, rows, dh), F32),
        compiler_params=_params("parallel", "arbitrary", "arbitrary"),
        name="moba_sample",
    )(page_table, q_rows, kn_rows, vn_rows, *([cache_k] * g), *([cache_v] * g))
    return o.reshape(b, n_heads, n_new, dh).transpose(0, 2, 1, 3).reshape(b, n_new, width)


def _hgrn_levels():
    sizes, h = [], HG_SUB
    while h < HG_CHUNK:
        sizes.append(h)
        h *= 2
    return sizes


def _hgrn_prompt_kernel(xq_ref, xf_ref, xi_ref, xg_ref, lb_ref, gn_ref, s0_ref, o_ref, s_ref,
                        st_scr, st_old, tril_scr, group_scr, diag_scr):
    t = pl.program_id(2)
    n = xq_ref.shape[0]
    dh = HEAD_DIM
    c = HG_CHUNK
    levels = _hgrn_levels()

    @pl.when(t == 0)
    def _():
        for hh in range(HG_HEADS_PER_STEP):
            st_scr[hh] = s0_ref[hh].T
        r = lax.broadcasted_iota(jnp.int32, (n, n), 0)
        cc = lax.broadcasted_iota(jnp.int32, (n, n), 1)
        tril_scr[...] = jnp.where((cc <= r) & (r // c == cc // c), 1.0, 0.0).astype(BF16)
        diag_scr[...] = jnp.where((cc <= r) & (r // HG_SUB == cc // HG_SUB), 1.0, 0.0)
        for li, hs in enumerate(levels):
            group_scr[li] = jnp.where(r // (2 * hs) == cc // (2 * hs), 1.0, 0.0)

    heads = range(HG_HEADS_PER_STEP)
    cols = [slice(hh * dh, (hh + 1) * dh) for hh in heads]

    def finish(hh, o):
        y = o * lax.rsqrt(jnp.mean(o * o, axis=-1, keepdims=True) + EPS) * gn_ref[...]
        o_ref[:, cols[hh]] = (y * _sigmoid(xg_ref[:, cols[hh]])).astype(o_ref.dtype)

    qkg = [_hgrn_gates(xq_ref[:, cs], xf_ref[:, cs], lb_ref[:, cs]) for cs in cols]
    vb = [xi_ref[:, cs].astype(BF16) for cs in cols]
    b = [_hgrn_cumsum(g2, tril_scr) for _, _, g2 in qkg]
    grow = [_hgrn_tile_grow(bh) for bh in b]
    a = [_hgrn_level_weights(q, k, bh, group_scr) for (q, k, _), bh in zip(qkg, b)]
    ad = [_hgrn_tile_weights(q, k, jnp.minimum(gh, HG_MAX_GROW), diag_scr) for (q, k, _), gh in zip(qkg, grow)]
    o = [jnp.dot((ah + adh).astype(BF16), vh, preferred_element_type=F32) for ah, adh, vh in zip(a, ad, vb)]
    for hh in heads:
        st_old[hh] = st_scr[hh]
        oh, st = _hgrn_carry(qkg[hh][0], qkg[hh][1], b[hh], vb[hh], o[hh], st_scr[hh])
        st_scr[hh] = st
        finish(hh, oh)

    for hh in heads:
        @pl.when(jnp.max(grow[hh]) > HG_MAX_GROW)
        def _(hh=hh):
            cs = cols[hh]
            q, k, g2 = _hgrn_gates(xq_ref[:, cs], xf_ref[:, cs], lb_ref[:, cs])
            v = xi_ref[:, cs]
            bh = _hgrn_cumsum(g2, tril_scr)
            o_lv = jnp.dot(_hgrn_level_weights(q, k, bh, group_scr).astype(BF16), v.astype(BF16),
                           preferred_element_type=F32)
            oh, _ = _hgrn_carry(q, k, bh, v.astype(BF16), o_lv + _hgrn_tile_pairs(q, k, bh, v), st_old[hh])
            finish(hh, oh)

    @pl.when(t == pl.num_programs(2) - 1)
    def _():
        for hh in heads:
            s_ref[hh] = st_scr[hh].T


def _hgrn_gates(xq, xf, lb):
    f = lb + (1.0 - lb) * _sigmoid(xf)
    return xq * _sigmoid(xq), 1.0 - f, jnp.log2(f)


def _hgrn_cumsum(g2, tril_scr):
    b = None
    resid = g2
    for _ in range(3):
        part = resid.astype(BF16)
        resid = resid - part.astype(F32)
        term = jnp.dot(tril_scr[...], part, preferred_element_type=F32)
        b = term if b is None else b + term
    return b


def _hgrn_level_weights(q, k, b, group_scr):
    n, dh = q.shape
    a = None
    for li, hs in enumerate(_hgrn_levels()):
        zero = jnp.zeros((hs, dh), F32)
        q_rows, k_rows = [], []
        for g0 in range(0, n, 2 * hs):
            first, second = slice(g0, g0 + hs), slice(g0 + hs, g0 + 2 * hs)
            bref = b[g0 + hs - 1:g0 + hs, :]
            q_rows += [zero, q[second] * jnp.exp2(b[second] - bref)]
            k_rows += [k[first] * jnp.exp2(bref - b[first]), zero]
        ql = jnp.concatenate(q_rows, axis=0).astype(BF16)
        kl = jnp.concatenate(k_rows, axis=0).astype(BF16)
        al = lax.dot_general(ql, kl, _NT, preferred_element_type=F32) * group_scr[li]
        a = al if a is None else a + al
    return a


def _hgrn_tile_grow(b):
    n, dh = b.shape
    zero_row = jnp.zeros((1, dh), F32)
    b0 = jnp.concatenate([jnp.broadcast_to(zero_row if r0 % HG_CHUNK == 0 else b[r0 - 1:r0, :], (HG_SUB, dh))
                          for r0 in range(0, n, HG_SUB)], axis=0)
    return b0 - b


def _hgrn_tile_weights(q, k, grow, diag_scr):
    qd = (q * jnp.exp2(-grow)).astype(BF16)
    kd = (k * jnp.exp2(grow)).astype(BF16)
    return lax.dot_general(qd, kd, _NT, preferred_element_type=F32) * diag_scr[...]


def _hgrn_tile_pairs(q, k, b, v):
    n, dh = q.shape
    sub = HG_SUB
    q3, k3, b3, v3 = (x.reshape(n // sub, sub, dh) for x in (q, k, b, v))
    tio = lax.broadcasted_iota(jnp.int32, q3.shape, 1)
    od = jnp.zeros(q3.shape, F32)
    for s in range(sub):
        e = jnp.exp2(jnp.where(tio >= s, b3 - b3[:, s:s + 1, :], MASK_FILL))
        w = q3 * k3[:, s:s + 1, :] * e
        od = od + jnp.sum(w, axis=-1, keepdims=True) * v3[:, s:s + 1, :]
    return od.reshape(n, dh)


def _hgrn_carry(q, k, b, vb, o, st):
    n = q.shape[0]
    c = HG_CHUNK
    parts = []
    for c0 in range(0, n, c):
        rows = slice(c0, c0 + c)
        bc = b[rows]
        b_end = bc[c - 1:c, :]
        qd = (q[rows] * jnp.exp2(bc)).astype(BF16)
        parts.append(o[rows] + lax.dot_general(qd, st.astype(BF16), _NT, preferred_element_type=F32))
        kd = (k[rows] * jnp.exp2(b_end - bc)).astype(BF16)
        st = st * jnp.exp2(b_end) + lax.dot_general(vb[rows], kd, _TN, preferred_element_type=F32)
    return jnp.concatenate(parts, axis=0), st


def _hgrn_prompt(p3, lb, gn, s0, n_heads, col0):
    b, l, _ = p3.shape
    dh = HEAD_DIM
    step = min(HG_STEP, l)
    assert l % step == 0 and step % HG_CHUNK == 0
    hps = HG_HEADS_PER_STEP
    assert n_heads % hps == 0 and col0 % (hps * dh) == 0
    c0 = col0 // (hps * dh)
    spec = lambda off: pl.BlockSpec((None, step, hps * dh), lambda bb, h, t: (bb, t, c0 + off * (n_heads // hps) + h))
    vec = pl.BlockSpec((1, hps * dh), lambda bb, h, t: (0, h))
    state = pl.BlockSpec((None, hps, dh, dh), lambda bb, h, t: (bb, h, 0, 0))
    return pl.pallas_call(
        _hgrn_prompt_kernel,
        grid=(b, n_heads // hps, l // step),
        in_specs=[spec(0), spec(1), spec(2), spec(3), vec, pl.BlockSpec((1, dh), lambda bb, h, t: (0, 0)), state],
        out_specs=[pl.BlockSpec((None, step, hps * dh), lambda bb, h, t: (bb, t, h)), state],
        out_shape=[jax.ShapeDtypeStruct((b, l, n_heads * dh), BF16), jax.ShapeDtypeStruct(s0.shape, F32)],
        scratch_shapes=[pltpu.VMEM((hps, dh, dh), F32),
                        pltpu.VMEM((hps, dh, dh), F32),
                        pltpu.VMEM((step, step), BF16),
                        pltpu.VMEM((len(_hgrn_levels()), step, step), F32),
                        pltpu.VMEM((step, step), F32)],
        compiler_params=_params("parallel", "parallel", "arbitrary"),
        name="hgrn_prompt",
    )(p3, p3, p3, p3, lb.reshape(1, -1), gn.reshape(1, dh), s0)


def _hgrn_sample_kernel(xq_ref, xf_ref, xi_ref, xg_ref, lb_ref, gn_ref, s0_ref, o_ref, s_ref):
    n_new = xq_ref.shape[0]
    dh = HEAD_DIM
    pad = jnp.zeros((8 - n_new % 8, dh), F32)
    cols = lambda a: jnp.concatenate([a, pad], axis=0).T
    for h in range(s0_ref.shape[0]):
        cs = slice(h * dh, (h + 1) * dh)
        xq = xq_ref[:, cs]
        f = lb_ref[:, cs] + (1.0 - lb_ref[:, cs]) * _sigmoid(xf_ref[:, cs])
        q = xq * _sigmoid(xq)
        f_c, k_c, q_c = cols(f), cols(1.0 - f), cols(q)
        v = xi_ref[:, cs]
        s = s0_ref[h]
        outs = []
        for t in range(n_new):
            s = f_c[:, t:t + 1] * s + k_c[:, t:t + 1] * v[t:t + 1, :]
            outs.append(jnp.sum(q_c[:, t:t + 1] * s, axis=0, keepdims=True))
        o = jnp.concatenate(outs, axis=0)
        y = o * lax.rsqrt(jnp.mean(o * o, axis=-1, keepdims=True) + EPS) * gn_ref[...]
        o_ref[:, cs] = y * _sigmoid(xg_ref[:, cs])
        s_ref[h] = s


def _hgrn_sample(p3, lb, gn, state_all, layer, n_heads, col0):
    b, n_new, _ = p3.shape
    dh = HEAD_DIM
    width = n_heads * dh
    assert col0 % width == 0
    spec = lambda off: pl.BlockSpec((None, n_new, width), lambda bb: (bb, 0, col0 // width + off))
    return pl.pallas_call(
        _hgrn_sample_kernel,
        grid=(b,),
        in_specs=[spec(0), spec(1), spec(2), spec(3),
                  pl.BlockSpec((1, width), lambda bb: (0, 0)), pl.BlockSpec((1, dh), lambda bb: (0, 0)),
                  pl.BlockSpec((None, None, n_heads, dh, dh), lambda bb: (layer, bb, 0, 0, 0))],
        out_specs=[pl.BlockSpec((None, n_new, width), lambda bb: (bb, 0, 0)),
                   pl.BlockSpec((None, n_heads, dh, dh), lambda bb: (bb, 0, 0, 0))],
        out_shape=[jax.ShapeDtypeStruct((b, n_new, width), F32),
                   jax.ShapeDtypeStruct(state_all.shape[1:], F32)],
        compiler_params=_params("parallel"),
        name="hgrn_sample",
    )(p3, p3, p3, p3, lb.reshape(1, -1), gn.reshape(1, dh), state_all)


def _mem_attn_kernel(q_ref, k_ref, v_ref, g_ref, o_ref, *, scale):
    dm = g_ref.shape[1]
    for h in range(q_ref.shape[1] // dm):
        cols = slice(h * dm, (h + 1) * dm)
        q = q_ref[:, cols]
        qn = q * lax.rsqrt(jnp.mean(q * q, axis=-1, keepdims=True) + EPS) * g_ref[...]
        s = lax.dot_general(qn.astype(BF16), k_ref[:, cols].astype(BF16), _NT, preferred_element_type=F32) * scale
        p = jnp.exp(s - jnp.max(s, axis=-1, keepdims=True))
        o = jnp.dot(p.astype(BF16), v_ref[:, cols].astype(BF16), preferred_element_type=F32)
        o_ref[:, cols] = (o / jnp.sum(p, axis=-1, keepdims=True)).astype(o_ref.dtype)


def _mem_attn(p3, mem_k, mem_v, layer, gn, col0, tq, out_dtype):
    b, l, _ = p3.shape
    mtok, width = mem_k.shape[2], mem_k.shape[3]
    dm = width // MEM_HEADS
    kern = functools.partial(_mem_attn_kernel, scale=dm ** -0.5)
    kv = pl.BlockSpec((None, None, mtok, width), lambda bb, i: (layer, bb, 0, 0))
    return pl.pallas_call(
        kern,
        grid=(b, l // tq),
        in_specs=[pl.BlockSpec((None, tq, width), lambda bb, i: (bb, i, col0 // width)), kv, kv,
                  pl.BlockSpec((1, dm), lambda bb, i: (0, 0))],
        out_specs=pl.BlockSpec((None, tq, width), lambda bb, i: (bb, i, 0)),
        out_shape=jax.ShapeDtypeStruct((b, l, width), out_dtype),
        compiler_params=_params("parallel", "arbitrary"),
        name="mem_attn",
    )(p3, mem_k, mem_v, gn.reshape(1, dm))


def _merge_kernel(oa_ref, oh_ref, om_ref, ga_ref, gh_ref, gm_ref, wa_ref, wh_ref, wm_ref, o_ref):
    acc = _sigmoid(ga_ref[...]) * jnp.dot(oa_ref[...].astype(BF16), wa_ref[...], preferred_element_type=F32)
    acc = acc + _sigmoid(gh_ref[...]) * jnp.dot(oh_ref[...].astype(BF16), wh_ref[...], preferred_element_type=F32)
    acc = acc + _sigmoid(gm_ref[...]) * jnp.dot(om_ref[...].astype(BF16), wm_ref[...], preferred_element_type=F32)
    o_ref[...] = acc.astype(o_ref.dtype)


def _merge(o_a, o_h, o_m, p2, gate_col0, w_branch, layer, tm, tn):
    n, bw = o_a.shape
    d = w_branch.shape[3]
    gspec = lambda br: pl.BlockSpec((tm, tn), lambda i, j: (i, (gate_col0 + br * d) // tn + j))
    wspec = lambda br: pl.BlockSpec((None, None, bw, tn), lambda i, j: (layer, br, 0, j))
    ospec = pl.BlockSpec((tm, bw), lambda i, j: (i, 0))
    return pl.pallas_call(
        _merge_kernel,
        grid=(n // tm, d // tn),
        in_specs=[ospec, ospec, ospec, gspec(0), gspec(1), gspec(2), wspec(0), wspec(1), wspec(2)],
        out_specs=pl.BlockSpec((tm, tn), lambda i, j: (i, j)),
        out_shape=jax.ShapeDtypeStruct((n, d), BF16),
        compiler_params=_params("parallel", "arbitrary"),
        name="merge",
    )(o_a, o_h, o_m, p2, p2, p2, w_branch, w_branch, w_branch)


def _out_proj_kernel(m_ref, w_ref, x_ref, g_ref, y_ref, h_ref):
    y = x_ref[...] + jnp.dot(m_ref[...], w_ref[...], preferred_element_type=F32)
    y_ref[...] = y
    hn = y * lax.rsqrt(jnp.mean(y * y, axis=-1, keepdims=True) + EPS)
    h_ref[...] = (hn * g_ref[...]).astype(h_ref.dtype)


def _out_proj(m, w_out, layer, x, g, tm):
    n, d = x.shape
    row = pl.BlockSpec((tm, d), lambda i: (i, 0))
    return pl.pallas_call(
        _out_proj_kernel,
        grid=(n // tm,),
        in_specs=[row, pl.BlockSpec((None, d, d), lambda i: (layer, 0, 0)), row, pl.BlockSpec((1, d), lambda i: (0, 0))],
        out_specs=[row, row],
        out_shape=[jax.ShapeDtypeStruct((n, d), F32), jax.ShapeDtypeStruct((n, d), BF16)],
        compiler_params=_params("parallel"),
        name="out_proj",
    )(m, w_out, x, g.reshape(1, d))


def _mlp_kernel(h_ref, wu_ref, wd_ref, x_ref, g_ref, y_ref, hn_ref, acc):
    f = pl.program_id(1)

    @pl.when(f == 0)
    def _():
        acc[...] = jnp.zeros(acc.shape, F32)

    u = jnp.maximum(jnp.dot(h_ref[...], wu_ref[...], preferred_element_type=F32), 0.0)
    acc[...] += jnp.dot((u * u).astype(BF16), wd_ref[...], preferred_element_type=F32)

    @pl.when(f == pl.num_programs(1) - 1)
    def _():
        y = x_ref[...] + acc[...]
        y_ref[...] = y
        hn = y * lax.rsqrt(jnp.mean(y * y, axis=-1, keepdims=True) + EPS)
        hn_ref[...] = (hn * g_ref[...]).astype(hn_ref.dtype)


def _mlp(h, w_up, w_down, layer, x, g_next, tm, tf):
    n, d = x.shape
    dff = w_up.shape[2]
    row = pl.BlockSpec((tm, d), lambda i, f: (i, 0))
    return pl.pallas_call(
        _mlp_kernel,
        grid=(n // tm, dff // tf),
        in_specs=[row, pl.BlockSpec((None, d, tf), lambda i, f: (layer, 0, f)),
                  pl.BlockSpec((None, tf, d), lambda i, f: (layer, f, 0)), row,
                  pl.BlockSpec((1, d), lambda i, f: (0, 0))],
        out_specs=[row, row],
        out_shape=[jax.ShapeDtypeStruct((n, d), F32), jax.ShapeDtypeStruct((n, d), BF16)],
        scratch_shapes=[pltpu.VMEM((tm, d), F32)],
        compiler_params=_params("parallel", "arbitrary"),
        name="mlp",
    )(h, w_up, w_down, x, g_next.reshape(1, d))


def _tiles(n):
    return (min(n, 1024), min(n, 1024), min(n, 256), min(n, 512))


def _trunk_layer(layer, x, h, shape, w, lb, next_gain, attn_fn, hgrn_fn, mem_fn):
    b, l = shape
    n, d = x.shape
    t_proj, t_merge, t_out, t_mlp = _tiles(n)
    n_heads = w["hg_width"] // HEAD_DIM
    width = n_heads * HEAD_DIM
    qk_gain = jnp.concatenate([jnp.tile(w["q_norm_moba"][layer], n_heads), jnp.tile(w["k_norm_moba"][layer], n_heads)])
    if n >= width:
        p2, (k_new, v_new), by_head = _proj(h, w["w_in"], layer, qk_gain.reshape(1, -1), HEAD_DIM, t_proj, width,
                                            n_copies=2)
    else:
        p2, _, by_head = _proj(h, w["w_in"], layer, qk_gain.reshape(1, -1), HEAD_DIM, t_proj, 2 * width)
        k_new, v_new = p2[:, width:2 * width], p2[:, 2 * width:3 * width]
    kv = lambda a: a.reshape(b, l, n_heads, HEAD_DIM)
    p3 = p2.reshape(b, l, -1)
    o_a = attn_fn(p3, by_head)
    o_h, s_new = hgrn_fn(p3, lb)
    o_m = mem_fn(p3)
    flat = lambda a: a.reshape(n, -1)
    m = _merge(flat(o_a), flat(o_h), flat(o_m), p2, w["gate_col0"], w["w_branch"], layer, t_merge, 512)
    x, h2 = _out_proj(m, w["w_out"], layer, x, w["ln_mlp"][layer], t_out)
    x, h_next = _mlp(h2, w["w_up"], w["w_down"], layer, x, next_gain, t_mlp, 1024)
    return x, h_next, kv(k_new), kv(v_new), s_new


def kernel(x_prompt, x_sample, mem_prompt, cache_moba_k, cache_moba_v, cache_mem_k, cache_mem_v, state_hgrn,
           page_table, ln_mix, w_in, q_norm_moba, k_norm_moba, lb_logits, hg_out_norm, q_norm_mem, ln_mem,
           w_mem_kv, k_norm_mem, w_branch, w_out, ln_mlp, w_up, w_down):
    depth = w_in.shape[0]
    bp, lp, d = x_prompt.shape
    bs, ls, _ = x_sample.shape
    hg_width = lb_logits.shape[1]
    n_heads = hg_width // HEAD_DIM
    moba_width = n_heads * HEAD_DIM
    mem_width = cache_mem_k.shape[3] * cache_mem_k.shape[4]
    mem_tokens = mem_prompt.shape[1]
    hg_col0 = 3 * moba_width
    mem_col0 = hg_col0 + 4 * hg_width
    gate_col0 = mem_col0 + mem_width

    w = dict(
        w_in=w_in.astype(BF16), w_branch=w_branch.astype(BF16), w_out=w_out.astype(BF16),
        w_up=w_up.astype(BF16), w_down=w_down.astype(BF16), ln_mlp=ln_mlp,
        q_norm_moba=q_norm_moba, k_norm_moba=k_norm_moba, hg_width=hg_width, gate_col0=gate_col0,
    )
    w_mem = w_mem_kv.astype(BF16)
    lower = _lower_bounds(lb_logits)

    ck = cache_moba_k.reshape(cache_moba_k.shape[:2] + (-1, HEAD_DIM))
    cv = cache_moba_v.reshape(cache_moba_v.shape[:2] + (-1, HEAD_DIM))
    cmk = cache_mem_k.reshape(cache_mem_k.shape[:3] + (mem_width,))
    cmv = cache_mem_v.reshape(cache_mem_v.shape[:3] + (mem_width,))

    xp = x_prompt.reshape(bp * lp, d)
    xs = x_sample.reshape(bs * ls, d)
    hp = _rmsnorm_bf16(xp, ln_mix[0], min(bp * lp, 512))
    hs = _rmsnorm_bf16(xs, ln_mix[0], bs * ls)
    s0_p = jnp.zeros((bp,) + state_hgrn.shape[2:], F32)
    mem_flat = mem_prompt.reshape(bp * mem_tokens, d)

    kp_l, vp_l, mkp_l, mvp_l, sp_l, ks_l, vs_l, ss_l = [], [], [], [], [], [], [], []
    for layer in range(depth):
        next_gain = ln_mix[layer + 1] if layer + 1 < depth else ln_mix[layer]
        gn_h = hg_out_norm[layer]
        gn_m = q_norm_mem[layer]

        hm = _rmsnorm_bf16(mem_flat, ln_mem[layer], min(bp * mem_tokens, 512))
        mem_gain = jnp.tile(k_norm_mem[layer], MEM_HEADS).reshape(1, -1)
        mkv, _, _ = _proj(hm, w_mem, layer, mem_gain, mem_width // MEM_HEADS, min(bp * mem_tokens, 512), mem_width)
        mk = mkv[:, :mem_width].reshape(1, bp, mem_tokens, mem_width)
        mv = mkv[:, mem_width:].reshape(1, bp, mem_tokens, mem_width)

        xp, hp, kp, vp, sp = _trunk_layer(
            layer, xp, hp, (bp, lp), w, lower[layer], next_gain,
            lambda p3, by_head: _moba_prompt(p3, by_head[0], by_head[1], n_heads),
            lambda p3, lb: _hgrn_prompt(p3, lb, gn_h, s0_p, n_heads, hg_col0),
            lambda p3: _mem_attn(p3, mk, mv, 0, gn_m, mem_col0, min(lp, 512), BF16))
        xs, hs, ks, vs, ss = _trunk_layer(
            layer, xs, hs, (bs, ls), w, lower[layer], next_gain,
            lambda p3, by_head: _moba_sample(p3, ck, cv, layer, page_table, n_heads),
            lambda p3, lb: _hgrn_sample(p3, lb, gn_h, state_hgrn, layer, n_heads, hg_col0),
            lambda p3: _mem_attn(p3, cmk, cmv, layer, gn_m, mem_col0, ls, F32))

        kp_l.append(kp)
        vp_l.append(vp)
        ks_l.append(ks)
        vs_l.append(vs)
        mem_shape = (bp, mem_tokens, MEM_HEADS, mem_width // MEM_HEADS)
        mkp_l.append(mk.reshape(mem_shape))
        mvp_l.append(mv.reshape(mem_shape))
        sp_l.append(sp)
        ss_l.append(ss)

    return (xp.reshape(bp, lp, d), xs.reshape(bs, ls, d), jnp.stack(kp_l), jnp.stack(vp_l), jnp.stack(mkp_l),
            jnp.stack(mvp_l), jnp.stack(sp_l), jnp.stack(ks_l), jnp.stack(vs_l), jnp.stack(ss_l))
```
